```python
import jax, jax.numpy as jnp
from jax import lax
import numpy as np

D_MODEL = 4096
BATCH = 8
SEQ = 4096
DEPTH = 1

N_META = 16
HEAD_DIM = 64
N_Q_HEADS = 32
N_KV_HEADS = 4
GROUP = N_Q_HEADS // N_KV_HEADS
WINDOW = 128
BLOCK = 128
ROPE_THETA = 10000.0
Q_DIM = N_Q_HEADS * HEAD_DIM
KV_DIM = N_KV_HEADS * HEAD_DIM
CONV_DIM = D_MODEL // 2
CONV_WIDTH = 31
FFN_DIM = ((8 * D_MODEL + 3 * 256 - 1) // (3 * 256)) * 256
IN_DIM = Q_DIM + 2 * KV_DIM + 2 * CONV_DIM + 2 * D_MODEL
EPS = 1e-6

kernel_name = "hybrid_swa_sink_conformer_conv_gated_block"


def rms_norm(x, g):
    xf = x.astype(jnp.float32)
    y = xf * lax.rsqrt(jnp.mean(xf * xf, axis=-1, keepdims=True) + EPS)
    return (y * g.astype(jnp.float32)).astype(x.dtype)


def layer_norm(x, g, b):
    xf = x.astype(jnp.float32)
    mu = jnp.mean(xf, axis=-1, keepdims=True)
    xc = xf - mu
    y = xc * lax.rsqrt(jnp.mean(xc * xc, axis=-1, keepdims=True) + EPS)
    return (y * g.astype(jnp.float32) + b.astype(jnp.float32)).astype(x.dtype)


def rope_tables(length):
    pos = jnp.arange(length, dtype=jnp.float32)
    inv_freq = ROPE_THETA ** (-jnp.arange(0, HEAD_DIM, 2, dtype=jnp.float32) / HEAD_DIM)
    ang = pos[:, None] * inv_freq[None, :]
    return jnp.cos(ang), jnp.sin(ang)


def apply_rope(x, cos, sin):
    xf = x.astype(jnp.float32)
    x1, x2 = jnp.split(xf, 2, axis=-1)
    c = cos[None, :, None, :]
    s = sin[None, :, None, :]
    return jnp.concatenate([x1 * c - x2 * s, x2 * c + x1 * s], axis=-1).astype(x.dtype)


def sliding_window_attention(q, k, v, sinks):
    b, length = q.shape[0], q.shape[1]
    pad = BLOCK - N_META
    padded = length + pad
    nb = padded // BLOCK
    padw = ((0, 0), (pad, 0), (0, 0), (0, 0))
    qb = jnp.pad(q, padw).reshape(b, nb, BLOCK, N_KV_HEADS, GROUP, HEAD_DIM)
    kb = jnp.pad(k, padw).reshape(b, nb, BLOCK, N_KV_HEADS, HEAD_DIM)
    vb = jnp.pad(v, padw).reshape(b, nb, BLOCK, N_KV_HEADS, HEAD_DIM)
    k_prev = jnp.concatenate([jnp.zeros_like(kb[:, :1]), kb[:, :-1]], axis=1)
    v_prev = jnp.concatenate([jnp.zeros_like(vb[:, :1]), vb[:, :-1]], axis=1)
    k_band = jnp.concatenate([k_prev, kb], axis=2)
    v_band = jnp.concatenate([v_prev, vb], axis=2)
    k_meta = k[:, :N_META]
    v_meta = v[:, :N_META]
    scale = HEAD_DIM ** -0.5

    s_band = jnp.einsum('bnqhgd,bnkhd->bnhgqk', qb, k_band).astype(jnp.float32) * scale
    s_meta = jnp.einsum('bnqhgd,bmhd->bnhgqm', qb, k_meta).astype(jnp.float32) * scale

    blk = jnp.arange(nb)[:, None]
    q_pos = blk * BLOCK + jnp.arange(BLOCK)[None, :] - pad
    k_pos = (blk - 1) * BLOCK + jnp.arange(2 * BLOCK)[None, :] - pad
    qp = q_pos[:, :, None]
    kp = k_pos[:, None, :]
    band_mask = (kp >= N_META) & (kp <= qp) & (qp - kp < WINDOW)
    meta_mask = jnp.arange(N_META)[None, None, :] <= qp

    neg = jnp.float32(-jnp.inf)
    s_band = jnp.where(band_mask[None, :, None, None], s_band, neg)
    s_meta = jnp.where(meta_mask[None, :, None, None], s_meta, neg)
    sink = jnp.broadcast_to(
        sinks.astype(jnp.float32).reshape(N_KV_HEADS, GROUP)[None, None, :, :, None, None],
        s_band.shape[:-1] + (1,))
    probs = jax.nn.softmax(jnp.concatenate([s_band, s_meta, sink], axis=-1), axis=-1)
    p_band = probs[..., :2 * BLOCK].astype(v.dtype)
    p_meta = probs[..., 2 * BLOCK:2 * BLOCK + N_META].astype(v.dtype)
    o = (jnp.einsum('bnhgqk,bnkhd->bnqhgd', p_band, v_band)
         + jnp.einsum('bnhgqm,bmhd->bnqhgd', p_meta, v_meta))
    return o.reshape(b, padded, Q_DIM)[:, pad:]


def conformer_conv(c_in, conv_w, conv_b, ln_g, ln_b, w_co, b_co):
    a, g = jnp.split(c_in, 2, axis=-1)
    c = a * jax.nn.sigmoid(g)
    c = lax.conv_general_dilated(
        c, conv_w.astype(c.dtype), window_strides=(1,), padding=[(CONV_WIDTH - 1, 0)],
        dimension_numbers=('NWC', 'WIO', 'NWC'), feature_group_count=CONV_DIM) + conv_b
    c = layer_norm(c, ln_g, ln_b)
    c = c * jax.nn.sigmoid(c)
    return c @ w_co + b_co


def mixer_block(u, cos, sin, w_in, b_in, sinks, conv_w, conv_b, ln_g, ln_b,
                w_ao, w_co, b_co, w_out):
    b, length, _ = u.shape
    z = u @ w_in + b_in
    idx = np.cumsum([Q_DIM, KV_DIM, KV_DIM, 2 * CONV_DIM, D_MODEL])
    q, k, v, c_in, gate_a, gate_b = jnp.split(z, idx, axis=-1)
    q = apply_rope(q.reshape(b, length, N_Q_HEADS, HEAD_DIM), cos, sin)
    k = apply_rope(k.reshape(b, length, N_KV_HEADS, HEAD_DIM), cos, sin)
    v = v.reshape(b, length, N_KV_HEADS, HEAD_DIM)
    branch_a = sliding_window_attention(q, k, v, sinks) @ w_ao
    branch_b = conformer_conv(c_in, conv_w, conv_b, ln_g, ln_b, w_co, b_co)
    merged = jax.nn.sigmoid(gate_a) * branch_a + jax.nn.sigmoid(gate_b) * branch_b
    return merged @ w_out


def swiglu(u, w_gate_up, w_down):
    gu = u @ w_gate_up
    g, up = jnp.split(gu, 2, axis=-1)
    return (jax.nn.silu(g) * up) @ w_down


def _fwd_setup_inputs(seed: int = 0) -> dict:
    key = jax.random.key(seed)
    ks = jax.random.split(key, 20)
    f32 = jnp.float32
    nrm = lambda k, shape, s: jax.random.normal(k, shape, f32) * s
    return {
        "x": nrm(ks[0], (BATCH, SEQ, D_MODEL), 1.0),
        "meta_tokens": nrm(ks[1], (N_META, D_MODEL), 1.0),
        "mix_norm_g": 1.0 + nrm(ks[2], (DEPTH, D_MODEL), 0.02),
        "w_in": nrm(ks[3], (DEPTH, D_MODEL, IN_DIM), D_MODEL ** -0.5),
        "b_in": nrm(ks[4], (DEPTH, IN_DIM), 0.02),
        "attn_sinks": nrm(ks[5], (DEPTH, N_Q_HEADS), 1.0),
        "conv_w": nrm(ks[6], (DEPTH, CONV_WIDTH, 1, CONV_DIM), CONV_WIDTH ** -0.5),
        "conv_b": nrm(ks[7], (DEPTH, CONV_DIM), 0.02),
        "conv_ln_g": 1.0 + nrm(ks[8], (DEPTH, CONV_DIM), 0.02),
        "conv_ln_b": nrm(ks[9], (DEPTH, CONV_DIM), 0.02),
        "w_attn_o": nrm(ks[10], (DEPTH, Q_DIM, D_MODEL), Q_DIM ** -0.5),
        "w_conv_o": nrm(ks[11], (DEPTH, CONV_DIM, D_MODEL), CONV_DIM ** -0.5),
        "b_conv_o": nrm(ks[12], (DEPTH, D_MODEL), 0.02),
        "w_out": nrm(ks[13], (DEPTH, D_MODEL, D_MODEL), D_MODEL ** -0.5),
        "ffn_norm_g": 1.0 + nrm(ks[14], (DEPTH, D_MODEL), 0.02),
        "w_gate_up": nrm(ks[15], (DEPTH, D_MODEL, 2 * FFN_DIM), D_MODEL ** -0.5),
        "w_down": nrm(ks[16], (DEPTH, FFN_DIM, D_MODEL), FFN_DIM ** -0.5),
        "final_norm_g": 1.0 + nrm(ks[17], (D_MODEL,), 0.02),
    }


def _fwd_reference(x, meta_tokens, mix_norm_g, w_in, b_in, attn_sinks, conv_w, conv_b,
              conv_ln_g, conv_ln_b, w_attn_o, w_conv_o, b_conv_o, w_out,
              ffn_norm_g, w_gate_up, w_down, final_norm_g):
    b = x.shape[0]
    meta = jnp.broadcast_to(meta_tokens[None].astype(x.dtype), (b, N_META, D_MODEL))
    h = jnp.concatenate([meta, x], axis=1)
    cos, sin = rope_tables(h.shape[1])
    for layer in range(DEPTH):
        u = rms_norm(h, mix_norm_g[layer])
        h = h + mixer_block(u, cos, sin, w_in[layer], b_in[layer], attn_sinks[layer],
                            conv_w[layer], conv_b[layer], conv_ln_g[layer], conv_ln_b[layer],
                            w_attn_o[layer], w_conv_o[layer], b_conv_o[layer], w_out[layer])
        h = h + swiglu(rms_norm(h, ffn_norm_g[layer]), w_gate_up[layer], w_down[layer])
    y = rms_norm(h, final_norm_g)
    return y[:, N_META:]


import jax as _jax
import jax.numpy as _jnp

TWIN_FORMAT = 'train_step'
FWD_PARAMS = ['x', 'meta_tokens', 'mix_norm_g', 'w_in', 'b_in', 'attn_sinks', 'conv_w', 'conv_b', 'conv_ln_g', 'conv_ln_b', 'w_attn_o', 'w_conv_o', 'b_conv_o', 'w_out', 'ffn_norm_g', 'w_gate_up', 'w_down', 'final_norm_g']
TWIN_WEIGHTS = ['meta_tokens', 'mix_norm_g', 'w_in', 'b_in', 'attn_sinks', 'conv_w', 'conv_b', 'conv_ln_g', 'conv_ln_b', 'w_attn_o', 'w_conv_o', 'b_conv_o', 'w_out', 'ffn_norm_g', 'w_gate_up', 'w_down', 'final_norm_g']
TWIN_DIFF_INPUT = 'x'
TWIN_INPUTS = ['x', 'meta_tokens', 'mix_norm_g', 'w_in', 'b_in', 'attn_sinks', 'conv_w', 'conv_b', 'conv_ln_g', 'conv_ln_b', 'w_attn_o', 'w_conv_o', 'b_conv_o', 'w_out', 'ffn_norm_g', 'w_gate_up', 'w_down', 'final_norm_g', 'loss_target', 'm_meta_tokens', 'm_mix_norm_g', 'm_w_in', 'm_b_in', 'm_attn_sinks', 'm_conv_w', 'm_conv_b', 'm_conv_ln_g', 'm_conv_ln_b', 'm_w_attn_o', 'm_w_conv_o', 'm_b_conv_o', 'm_w_out', 'm_ffn_norm_g', 'm_w_gate_up', 'm_w_down', 'm_final_norm_g', 'v_meta_tokens', 'v_mix_norm_g', 'v_w_in', 'v_b_in', 'v_attn_sinks', 'v_conv_w', 'v_conv_b', 'v_conv_ln_g', 'v_conv_ln_b', 'v_w_attn_o', 'v_w_conv_o', 'v_b_conv_o', 'v_w_out', 'v_ffn_norm_g', 'v_w_gate_up', 'v_w_down', 'v_final_norm_g']
TWIN_OUTPUTS = ['loss', 'grad_x', 'grad_meta_tokens', 'grad_mix_norm_g', 'grad_w_in', 'grad_b_in', 'grad_attn_sinks', 'grad_conv_w', 'grad_conv_b', 'grad_conv_ln_g', 'grad_conv_ln_b', 'grad_w_attn_o', 'grad_w_conv_o', 'grad_b_conv_o', 'grad_w_out', 'grad_ffn_norm_g', 'grad_w_gate_up', 'grad_w_down', 'grad_final_norm_g', 'delta_meta_tokens', 'delta_mix_norm_g', 'delta_w_in', 'delta_b_in', 'delta_attn_sinks', 'delta_conv_w', 'delta_conv_b', 'delta_conv_ln_g', 'delta_conv_ln_b', 'delta_w_attn_o', 'delta_w_conv_o', 'delta_b_conv_o', 'delta_w_out', 'delta_ffn_norm_g', 'delta_w_gate_up', 'delta_w_down', 'delta_final_norm_g', 'new_m_meta_tokens', 'new_m_mix_norm_g', 'new_m_w_in', 'new_m_b_in', 'new_m_attn_sinks', 'new_m_conv_w', 'new_m_conv_b', 'new_m_conv_ln_g', 'new_m_conv_ln_b', 'new_m_w_attn_o', 'new_m_w_conv_o', 'new_m_b_conv_o', 'new_m_w_out', 'new_m_ffn_norm_g', 'new_m_w_gate_up', 'new_m_w_down', 'new_m_final_norm_g', 'new_v_meta_tokens', 'new_v_mix_norm_g', 'new_v_w_in', 'new_v_b_in', 'new_v_attn_sinks', 'new_v_conv_w', 'new_v_conv_b', 'new_v_conv_ln_g', 'new_v_conv_ln_b', 'new_v_w_attn_o', 'new_v_w_conv_o', 'new_v_b_conv_o', 'new_v_w_out', 'new_v_ffn_norm_g', 'new_v_w_gate_up', 'new_v_w_down', 'new_v_final_norm_g']
TWIN_LEAF_KINDS = {'loss': 'loss', 'grad_x': 'grad_x', 'grad_meta_tokens': 'grad_w', 'grad_mix_norm_g': 'grad_w', 'grad_w_in': 'grad_w', 'grad_b_in': 'grad_w', 'grad_attn_sinks': 'grad_w', 'grad_conv_w': 'grad_w', 'grad_conv_b': 'grad_w', 'grad_conv_ln_g': 'grad_w', 'grad_conv_ln_b': 'grad_w', 'grad_w_attn_o': 'grad_w', 'grad_w_conv_o': 'grad_w', 'grad_b_conv_o': 'grad_w', 'grad_w_out': 'grad_w', 'grad_ffn_norm_g': 'grad_w', 'grad_w_gate_up': 'grad_w', 'grad_w_down': 'grad_w', 'grad_final_norm_g': 'grad_w', 'delta_meta_tokens': 'delta_w', 'delta_mix_norm_g': 'delta_w', 'delta_w_in': 'delta_w', 'delta_b_in': 'delta_w', 'delta_attn_sinks': 'delta_w', 'delta_conv_w': 'delta_w', 'delta_conv_b': 'delta_w', 'delta_conv_ln_g': 'delta_w', 'delta_conv_ln_b': 'delta_w', 'delta_w_attn_o': 'delta_w', 'delta_w_conv_o': 'delta_w', 'delta_b_conv_o': 'delta_w', 'delta_w_out': 'delta_w', 'delta_ffn_norm_g': 'delta_w', 'delta_w_gate_up': 'delta_w', 'delta_w_down': 'delta_w', 'delta_final_norm_g': 'delta_w', 'new_m_meta_tokens': 'new_m', 'new_m_mix_norm_g': 'new_m', 'new_m_w_in': 'new_m', 'new_m_b_in': 'new_m', 'new_m_attn_sinks': 'new_m', 'new_m_conv_w': 'new_m', 'new_m_conv_b': 'new_m', 'new_m_conv_ln_g': 'new_m', 'new_m_conv_ln_b': 'new_m', 'new_m_w_attn_o': 'new_m', 'new_m_w_conv_o': 'new_m', 'new_m_b_conv_o': 'new_m', 'new_m_w_out': 'new_m', 'new_m_ffn_norm_g': 'new_m', 'new_m_w_gate_up': 'new_m', 'new_m_w_down': 'new_m', 'new_m_final_norm_g': 'new_m', 'new_v_meta_tokens': 'new_v', 'new_v_mix_norm_g': 'new_v', 'new_v_w_in': 'new_v', 'new_v_b_in': 'new_v', 'new_v_attn_sinks': 'new_v', 'new_v_conv_w': 'new_v', 'new_v_conv_b': 'new_v', 'new_v_conv_ln_g': 'new_v', 'new_v_conv_ln_b': 'new_v', 'new_v_w_attn_o': 'new_v', 'new_v_w_conv_o': 'new_v', 'new_v_b_conv_o': 'new_v', 'new_v_w_out': 'new_v', 'new_v_ffn_norm_g': 'new_v', 'new_v_w_gate_up': 'new_v', 'new_v_w_down': 'new_v', 'new_v_final_norm_g': 'new_v'}


def _forward(args):
    return _fwd_reference(*[args[k] for k in FWD_PARAMS])


def _output_shape():
    out = _jax.eval_shape(lambda: _forward(_fwd_setup_inputs(0)))
    return out.shape, out.dtype

N_MICROBATCH = 1
ADAM_LR = 0.001
ADAM_B1 = 0.9
ADAM_B2 = 0.999
ADAM_EPS = 1e-08
ADAM_WD = 0.01
ADAM_STEP = 10
PER_EXAMPLE_BATCH_AXIS = {'x': 0, 'loss_target': 0}
SHARED_INPUTS = []
_WEIGHT_DTYPES = {'meta_tokens': _jnp.float32, 'mix_norm_g': _jnp.float32, 'w_in': _jnp.float32, 'b_in': _jnp.float32, 'attn_sinks': _jnp.float32, 'conv_w': _jnp.float32, 'conv_b': _jnp.float32, 'conv_ln_g': _jnp.float32, 'conv_ln_b': _jnp.float32, 'w_attn_o': _jnp.float32, 'w_conv_o': _jnp.float32, 'b_conv_o': _jnp.float32, 'w_out': _jnp.float32, 'ffn_norm_g': _jnp.float32, 'w_gate_up': _jnp.float32, 'w_down': _jnp.float32, 'final_norm_g': _jnp.float32}
MOMENT_SCALE = {'meta_tokens': 1.260885e-03, 'mix_norm_g': 1.945311e-02, 'w_in': 1.020649e-02, 'b_in': 1.837143e-02, 'attn_sinks': 1.580226e-03, 'conv_w': 2.259543e-02, 'conv_b': 4.145401e-02, 'conv_ln_g': 2.671026e-02, 'conv_ln_b': 2.271702e-02, 'w_attn_o': 4.388237e-03, 'w_conv_o': 1.543835e-02, 'b_conv_o': 2.647436e-02, 'w_out': 1.590722e-02, 'ffn_norm_g': 3.158709e-02, 'w_gate_up': 1.355439e-02, 'w_down': 2.182426e-02, 'final_norm_g': 7.983693e+00}


def _to_microbatches(a, axis):
    t = _jnp.moveaxis(a, axis, 0)
    t = t.reshape((N_MICROBATCH, t.shape[0] // N_MICROBATCH) + t.shape[1:])
    return _jnp.moveaxis(t, 1, axis + 1)


def setup_inputs(seed: int = 0) -> dict:
    inp = _fwd_setup_inputs(seed)
    key = _jax.random.fold_in(_jax.random.key(seed), 7919)
    shape, _ = _output_shape()
    out = dict(inp)
    out["loss_target"] = _jax.random.normal(_jax.random.fold_in(key, 0), shape, _jnp.float32)
    for i, name in enumerate(TWIN_WEIGHTS):
        w = inp[name].astype(_jnp.float32)
        if MOMENT_SCALE is None:
            s = _jnp.sqrt(_jnp.mean(_jnp.square(w)) + 1e-30)
        else:
            s = MOMENT_SCALE[name]
        km, kv = _jax.random.split(_jax.random.fold_in(key, i + 1))
        out[name] = w
        out["m_" + name] = s * _jax.random.normal(km, w.shape, _jnp.float32)
        out["v_" + name] = (s * s) * _jax.random.uniform(kv, w.shape, _jnp.float32, 0.5, 1.5)
    if N_MICROBATCH > 1:
        for name, axis in PER_EXAMPLE_BATCH_AXIS.items():
            out[name] = _to_microbatches(out[name], axis)
    return {'x': out['x'], 'meta_tokens': out['meta_tokens'], 'mix_norm_g': out['mix_norm_g'], 'w_in': out['w_in'], 'b_in': out['b_in'], 'attn_sinks': out['attn_sinks'], 'conv_w': out['conv_w'], 'conv_b': out['conv_b'], 'conv_ln_g': out['conv_ln_g'], 'conv_ln_b': out['conv_ln_b'], 'w_attn_o': out['w_attn_o'], 'w_conv_o': out['w_conv_o'], 'b_conv_o': out['b_conv_o'], 'w_out': out['w_out'], 'ffn_norm_g': out['ffn_norm_g'], 'w_gate_up': out['w_gate_up'], 'w_down': out['w_down'], 'final_norm_g': out['final_norm_g'], 'loss_target': out['loss_target'], 'm_meta_tokens': out['m_meta_tokens'], 'm_mix_norm_g': out['m_mix_norm_g'], 'm_w_in': out['m_w_in'], 'm_b_in': out['m_b_in'], 'm_attn_sinks': out['m_attn_sinks'], 'm_conv_w': out['m_conv_w'], 'm_conv_b': out['m_conv_b'], 'm_conv_ln_g': out['m_conv_ln_g'], 'm_conv_ln_b': out['m_conv_ln_b'], 'm_w_attn_o': out['m_w_attn_o'], 'm_w_conv_o': out['m_w_conv_o'], 'm_b_conv_o': out['m_b_conv_o'], 'm_w_out': out['m_w_out'], 'm_ffn_norm_g': out['m_ffn_norm_g'], 'm_w_gate_up': out['m_w_gate_up'], 'm_w_down': out['m_w_down'], 'm_final_norm_g': out['m_final_norm_g'], 'v_meta_tokens': out['v_meta_tokens'], 'v_mix_norm_g': out['v_mix_norm_g'], 'v_w_in': out['v_w_in'], 'v_b_in': out['v_b_in'], 'v_attn_sinks': out['v_attn_sinks'], 'v_conv_w': out['v_conv_w'], 'v_conv_b': out['v_conv_b'], 'v_conv_ln_g': out['v_conv_ln_g'], 'v_conv_ln_b': out['v_conv_ln_b'], 'v_w_attn_o': out['v_w_attn_o'], 'v_w_conv_o': out['v_w_conv_o'], 'v_b_conv_o': out['v_b_conv_o'], 'v_w_out': out['v_w_out'], 'v_ffn_norm_g': out['v_ffn_norm_g'], 'v_w_gate_up': out['v_w_gate_up'], 'v_w_down': out['v_w_down'], 'v_final_norm_g': out['v_final_norm_g']}


def _loss(weights, diff, rest, loss_target):
    with _jax.named_scope("forward"):
        args = {**rest, TWIN_DIFF_INPUT: diff, **{k: w.astype(_WEIGHT_DTYPES[k]) for k, w in weights.items()}}
        y = _forward(args)
    with _jax.named_scope("loss_head"):
        err = _jnp.square(y.astype(_jnp.float32) - loss_target)
        return 0.5 * _jnp.sum(_jnp.mean(err, axis=-1)) if err.ndim else 0.5 * err


def _adamw(w, g, m, v):
    m = ADAM_B1 * m + (1.0 - ADAM_B1) * g
    v = ADAM_B2 * v + (1.0 - ADAM_B2) * _jnp.square(g)
    m_hat = m / (1.0 - ADAM_B1 ** ADAM_STEP)
    v_hat = v / (1.0 - ADAM_B2 ** ADAM_STEP)
    delta = -ADAM_LR * (m_hat / (_jnp.sqrt(v_hat) + ADAM_EPS) + ADAM_WD * w)
    return delta, m, v


def reference(x, meta_tokens, mix_norm_g, w_in, b_in, attn_sinks, conv_w, conv_b, conv_ln_g, conv_ln_b, w_attn_o, w_conv_o, b_conv_o, w_out, ffn_norm_g, w_gate_up, w_down, final_norm_g, loss_target, m_meta_tokens, m_mix_norm_g, m_w_in, m_b_in, m_attn_sinks, m_conv_w, m_conv_b, m_conv_ln_g, m_conv_ln_b, m_w_attn_o, m_w_conv_o, m_b_conv_o, m_w_out, m_ffn_norm_g, m_w_gate_up, m_w_down, m_final_norm_g, v_meta_tokens, v_mix_norm_g, v_w_in, v_b_in, v_attn_sinks, v_conv_w, v_conv_b, v_conv_ln_g, v_conv_ln_b, v_w_attn_o, v_w_conv_o, v_b_conv_o, v_w_out, v_ffn_norm_g, v_w_gate_up, v_w_down, v_final_norm_g):
    given = dict(x=x, meta_tokens=meta_tokens, mix_norm_g=mix_norm_g, w_in=w_in, b_in=b_in, attn_sinks=attn_sinks, conv_w=conv_w, conv_b=conv_b, conv_ln_g=conv_ln_g, conv_ln_b=conv_ln_b, w_attn_o=w_attn_o, w_conv_o=w_conv_o, b_conv_o=b_conv_o, w_out=w_out, ffn_norm_g=ffn_norm_g, w_gate_up=w_gate_up, w_down=w_down, final_norm_g=final_norm_g, loss_target=loss_target, m_meta_tokens=m_meta_tokens, m_mix_norm_g=m_mix_norm_g, m_w_in=m_w_in, m_b_in=m_b_in, m_attn_sinks=m_attn_sinks, m_conv_w=m_conv_w, m_conv_b=m_conv_b, m_conv_ln_g=m_conv_ln_g, m_conv_ln_b=m_conv_ln_b, m_w_attn_o=m_w_attn_o, m_w_conv_o=m_w_conv_o, m_b_conv_o=m_b_conv_o, m_w_out=m_w_out, m_ffn_norm_g=m_ffn_norm_g, m_w_gate_up=m_w_gate_up, m_w_down=m_w_down, m_final_norm_g=m_final_norm_g, v_meta_tokens=v_meta_tokens, v_mix_norm_g=v_mix_norm_g, v_w_in=v_w_in, v_b_in=v_b_in, v_attn_sinks=v_attn_sinks, v_conv_w=v_conv_w, v_conv_b=v_conv_b, v_conv_ln_g=v_conv_ln_g, v_conv_ln_b=v_conv_ln_b, v_w_attn_o=v_w_attn_o, v_w_conv_o=v_w_conv_o, v_b_conv_o=v_b_conv_o, v_w_out=v_w_out, v_ffn_norm_g=v_ffn_norm_g, v_w_gate_up=v_w_gate_up, v_w_down=v_w_down, v_final_norm_g=v_final_norm_g)
    weights = {n: given[n] for n in TWIN_WEIGHTS}
    shared = {n: given[n] for n in SHARED_INPUTS}
    per_example = {n: given[n] for n in ['x']}
    grad_fn = _jax.value_and_grad(_loss, argnums=(0, 1))

    def one_microbatch(ex, loss_target):
        ex = dict(ex)
        diff = ex.pop(TWIN_DIFF_INPUT)
        return grad_fn(weights, diff, {**shared, **ex}, loss_target)

    if N_MICROBATCH == 1:
        loss, (grad_w, grad_x) = one_microbatch(per_example, given["loss_target"])
    else:
        def body(carry, xs):
            loss_sum, grad_sum = carry
            l_k, (gw_k, gx_k) = one_microbatch(xs[0], xs[1])
            with _jax.named_scope("update"):
                return (loss_sum + l_k, _jax.tree.map(_jnp.add, grad_sum, gw_k)), gx_k

        init = (_jnp.zeros((), _jnp.float32), _jax.tree.map(_jnp.zeros_like, weights))
        (loss, grad_w), grad_x = _jax.lax.scan(body, init, (per_example, given["loss_target"]))
    with _jax.named_scope("update"):
        delta_w, new_m, new_v = {}, {}, {}
        for n in TWIN_WEIGHTS:
            delta_w[n], new_m[n], new_v[n] = _adamw(weights[n], grad_w[n], given["m_" + n], given["v_" + n])
    return (loss, grad_x, *[grad_w[n] for n in TWIN_WEIGHTS], *[delta_w[n] for n in TWIN_WEIGHTS],
            *[new_m[n] for n in TWIN_WEIGHTS], *[new_v[n] for n in TWIN_WEIGHTS])
```

```python
import functools

import jax
import jax.numpy as jnp
from jax import lax
from jax.experimental import pallas as pl
from jax.experimental.pallas import tpu as pltpu

F32 = jnp.float32
BF16 = jnp.bfloat16
MESH = pl.DeviceIdType.MESH
ANY = pl.BlockSpec(memory_space=pl.ANY)

V7X_VMEM_BYTES = 64 * 1024 * 1024
VMEM_LIMIT = V7X_VMEM_BYTES - 8 * 1024 * 1024
LANES = 128
ATT_BLOCK = 128
WINDOW = 128
HEAD_DIM = 64
ROPE_THETA = 10000.0
EPS = 1e-6
CONV_WIDTH = 31
CONV_HALO = 32
N_DEV = 8

ADAM_LR = 0.001
ADAM_B1 = 0.9
ADAM_B2 = 0.999
ADAM_EPS = 1e-08
ADAM_WD = 0.01
ADAM_STEP = 10


def _tile(n, target, mult=LANES):
    best = None
    for t in range(mult, min(n, target) + 1, mult):
        if n % t == 0:
            best = t
    return best if best is not None else n


def _params(sem=None):
    return pltpu.CompilerParams(dimension_semantics=sem, vmem_limit_bytes=VMEM_LIMIT)


_DIMS = {"nn": (((1,), (0,)), ((), ())), "nt": (((1,), (1,)), ((), ())), "tn": (((0,), (0,)), ((), ()))}


def _mm(name, a, b, mode, M, N, C, *, a_off=(0, 0), b_off=(0, 0), bias=None, res=None,
        out_dtype=BF16, out=None, out_off=(0, 0), tm=None, tn=None, tc=None):
    wide = (res is not None) or jnp.dtype(out_dtype).itemsize == 4
    if mode == "tn":
        tm = tm or _tile(M, 1024)
        tn = tn or _tile(N, 1024 if tm <= 512 else 512)
        tc = tc or C
    else:
        tm = tm or (_tile(M, 1408, 8) if wide else M)
        tn = tn or _tile(N, 1024 if wide else 512)
        tc = tc or _tile(C, 1024)
    assert M % tm == 0 and N % tn == 0 and C % tc == 0, (name, M, N, C, tm, tn, tc)
    nk = C // tc
    (ar, ac), (br, bc), (orow, ocol) = a_off, b_off, out_off
    if mode == "nn":
        a_blk, b_blk = (tm, tc), (tc, tn)
        a_map = lambda i, j, k: (ar // tm + i, ac // tc + k)
        b_map = lambda i, j, k: (br // tc + k, bc // tn + j)
        assert ar % tm == 0 and ac % tc == 0 and br % tc == 0 and bc % tn == 0
    elif mode == "nt":
        a_blk, b_blk = (tm, tc), (tn, tc)
        a_map = lambda i, j, k: (ar // tm + i, ac // tc + k)
        b_map = lambda i, j, k: (br // tn + j, bc // tc + k)
        assert ar % tm == 0 and ac % tc == 0 and br % tn == 0 and bc % tc == 0
    else:
        a_blk, b_blk = (tc, tm), (tc, tn)
        a_map = lambda i, j, k: (ar // tc + k, ac // tm + i)
        b_map = lambda i, j, k: (br // tc + k, bc // tn + j)
        assert ar % tc == 0 and ac % tm == 0 and br % tc == 0 and bc % tn == 0
    assert orow % tm == 0 and ocol % tn == 0
    dims = _DIMS[mode]
    has_bias, has_res, has_out = bias is not None, res is not None, out is not None

    def body(*refs):
        refs = list(refs)
        a_ref, b_ref = refs[0], refs[1]
        pos = 2
        bias_ref = res_ref = None
        if has_bias:
            bias_ref = refs[pos]
            pos += 1
        if has_res:
            res_ref = refs[pos]
            pos += 1
        if has_out:
            pos += 1
        o_ref = refs[pos]
        acc_ref = refs[pos + 1] if nk > 1 else None

        def finish(r):
            if has_bias:
                r = r + bias_ref[...]
            if has_res:
                r = r + res_ref[...]
            o_ref[...] = r.astype(out_dtype)

        part = lax.dot_general(a_ref[...], b_ref[...], dims, preferred_element_type=F32)
        if nk == 1:
            finish(part)
        else:
            k = pl.program_id(2)

            @pl.when(k == 0)
            def _():
                acc_ref[...] = part

            @pl.when(k > 0)
            def _():
                acc_ref[...] += part

            @pl.when(k == nk - 1)
            def _():
                finish(acc_ref[...])

    in_specs = [pl.BlockSpec(a_blk, a_map), pl.BlockSpec(b_blk, b_map)]
    args = [a, b]
    if has_bias:
        in_specs.append(pl.BlockSpec((1, tn), lambda i, j, k: (0, j)))
        args.append(bias)
    if has_res:
        in_specs.append(pl.BlockSpec((tm, tn), lambda i, j, k: (i, j)))
        args.append(res)
    aliases = {}
    if has_out:
        in_specs.append(ANY)
        aliases = {len(args): 0}
        args.append(out)
        out_shape = jax.ShapeDtypeStruct(out.shape, out.dtype)
        assert out.dtype == out_dtype
    else:
        out_shape = jax.ShapeDtypeStruct((M, N), out_dtype)
    return pl.pallas_call(
        body, name=name, grid=(M // tm, N // tn, nk), in_specs=in_specs,
        out_specs=pl.BlockSpec((tm, tn), lambda i, j, k: (orow // tm + i, ocol // tn + j)),
        out_shape=out_shape, input_output_aliases=aliases,
        scratch_shapes=[pltpu.VMEM((tm, tn), F32)] if nk > 1 else [],
        compiler_params=_params(("parallel", "parallel", "arbitrary")),
    )(*args)


def _rms_fwd(name, h, g):
    T, D = h.shape
    tr = _tile(T, 384, 8)

    def body(h_ref, g_ref, u_ref):
        x = h_ref[...]
        r = lax.rsqrt(jnp.mean(x * x, axis=-1, keepdims=True) + EPS)
        u_ref[...] = (x * r * g_ref[...]).astype(BF16)

    return pl.pallas_call(
        body, name=name, grid=(T // tr,),
        in_specs=[pl.BlockSpec((tr, D), lambda i: (i, 0)), pl.BlockSpec((1, D), lambda i: (0, 0))],
        out_specs=pl.BlockSpec((tr, D), lambda i: (i, 0)),
        out_shape=jax.ShapeDtypeStruct((T, D), BF16), compiler_params=_params(("parallel",)),
    )(h, g)


def _rms_bwd(name, du, h, g, dres):
    T, D = h.shape
    tr = _tile(T, 128, 8)

    def body(du_ref, h_ref, g_ref, dres_ref, dh_ref, dhb_ref, dg_ref):
        x = h_ref[...]
        r = lax.rsqrt(jnp.mean(x * x, axis=-1, keepdims=True) + EPS)
        xh = x * r
        d = du_ref[...].astype(F32)
        dxh = d * g_ref[...]
        dx = r * (dxh - xh * jnp.mean(dxh * xh, axis=-1, keepdims=True))
        dh = dres_ref[...] + dx
        dh_ref[...] = dh
        dhb_ref[...] = dh.astype(BF16)
        part = jnp.sum(d * xh, axis=0, keepdims=True)

        @pl.when(pl.program_id(0) == 0)
        def _():
            dg_ref[...] = part

        @pl.when(pl.program_id(0) > 0)
        def _():
            dg_ref[...] += part

    row = pl.BlockSpec((tr, D), lambda i: (i, 0))
    vec = pl.BlockSpec((1, D), lambda i: (0, 0))
    return pl.pallas_call(
        body, name=name, grid=(T // tr,), in_specs=[row, row, vec, row], out_specs=[row, row, vec],
        out_shape=[jax.ShapeDtypeStruct((T, D), F32), jax.ShapeDtypeStruct((T, D), BF16),
                   jax.ShapeDtypeStruct((1, D), F32)],
        compiler_params=_params(("arbitrary",)),
    )(du, h, g, dres)


def _final_loss(h, g, tgt, row_lo, row_hi):
    T, D = h.shape
    tr = _tile(T, 128, 8)

    def body(h_ref, g_ref, t_ref, loss_ref, dh_ref, dhb_ref, dg_ref):
        i = pl.program_id(0)
        x = h_ref[...]
        r = lax.rsqrt(jnp.mean(x * x, axis=-1, keepdims=True) + EPS)
        xh = x * r
        gain = g_ref[...]
        rows = i * tr + lax.broadcasted_iota(jnp.int32, (tr, 1), 0)
        real = (rows >= row_lo) & (rows < row_hi)
        e = jnp.where(real, xh * gain - t_ref[...], 0.0)
        lpart = 0.5 * jnp.sum(jnp.mean(e * e, axis=-1, keepdims=True), axis=0, keepdims=True)
        dy = e * (1.0 / D)
        dxh = dy * gain
        dx = r * (dxh - xh * jnp.mean(dxh * xh, axis=-1, keepdims=True))
        dh_ref[...] = dx
        dhb_ref[...] = dx.astype(BF16)
        gpart = jnp.sum(dy * xh, axis=0, keepdims=True)

        @pl.when(i == 0)
        def _():
            dg_ref[...] = gpart
            loss_ref[...] = lpart

        @pl.when(i > 0)
        def _():
            dg_ref[...] += gpart
            loss_ref[...] += lpart

    row = pl.BlockSpec((tr, D), lambda i: (i, 0))
    vec = pl.BlockSpec((1, D), lambda i: (0, 0))
    one = pl.BlockSpec((1, 1), lambda i: (0, 0))
    return pl.pallas_call(
        body, name="final_loss", grid=(T // tr,), in_specs=[row, vec, row], out_specs=[one, row, row, vec],
        out_shape=[jax.ShapeDtypeStruct((1, 1), F32), jax.ShapeDtypeStruct((T, D), F32),
                   jax.ShapeDtypeStruct((T, D), BF16), jax.ShapeDtypeStruct((1, D), F32)],
        compiler_params=_params(("arbitrary",)),
    )(h, g, tgt)


def _colsum(name, a, width=None, col_off=0):
    T = a.shape[0]
    width = width or a.shape[1]
    cw = _tile(width, 512)
    tr = _tile(T, 384, 8)
    assert col_off % cw == 0

    def body(a_ref, o_ref):
        part = jnp.sum(a_ref[...].astype(F32), axis=0, keepdims=True)

        @pl.when(pl.program_id(1) == 0)
        def _():
            o_ref[...] = part

        @pl.when(pl.program_id(1) > 0)
        def _():
            o_ref[...] += part

    return pl.pallas_call(
        body, name=name, grid=(width // cw, T // tr),
        in_specs=[pl.BlockSpec((tr, cw), lambda j, i: (i, col_off // cw + j))],
        out_specs=pl.BlockSpec((1, cw), lambda j, i: (0, j)),
        out_shape=jax.ShapeDtypeStruct((1, width), F32), compiler_params=_params(("parallel", "arbitrary")),
    )(a)


def _rope_tables(T):
    pos = jnp.arange(T, dtype=F32)
    inv = ROPE_THETA ** (-jnp.arange(0, HEAD_DIM, 2, dtype=F32) / HEAD_DIM)
    ang = pos[:, None] * inv[None, :]
    c, s = jnp.cos(ang), jnp.sin(ang)
    reps = LANES // HEAD_DIM
    return (jnp.tile(jnp.concatenate([c, c], axis=1), (1, reps)),
            jnp.tile(jnp.concatenate([-s, s], axis=1), (1, reps)))


def _swap_halves(x):
    w = x.shape[1]
    half = HEAD_DIM // 2
    lane = lax.broadcasted_iota(jnp.int32, x.shape, 1)
    first = (lane & (HEAD_DIM - 1)) < half
    return jnp.where(first, pltpu.roll(x, w - half, 1), pltpu.roll(x, half, 1))


def _rope_fwd(z, cosf, sinf, Q, KV):
    T = z.shape[0]
    W = Q + KV
    tr = _tile(T, 384, 16)
    reps = W // LANES

    def body(z_ref, c_ref, s_ref, o_ref):
        x = z_ref[...].astype(F32)
        c = jnp.tile(c_ref[...], (1, reps))
        s = jnp.tile(s_ref[...], (1, reps))
        y = x * c + _swap_halves(x) * s
        lane = lax.broadcasted_iota(jnp.int32, y.shape, 1)
        y = jnp.where(lane < Q, y * (HEAD_DIM ** -0.5), y)
        o_ref[...] = y.astype(BF16)

    tab = pl.BlockSpec((tr, LANES), lambda i: (i, 0))
    return pl.pallas_call(
        body, name="rope_fwd", grid=(T // tr,),
        in_specs=[pl.BlockSpec((tr, W), lambda i: (i, 0)), tab, tab],
        out_specs=pl.BlockSpec((tr, W), lambda i: (i, 0)),
        out_shape=jax.ShapeDtypeStruct((T, W), BF16), compiler_params=_params(("parallel",)),
    )(z, cosf, sinf)


def _attn_mask(n, n_meta):
    B = ATT_BLOCK
    row = lax.broadcasted_iota(jnp.int32, (B, 3 * B), 0)
    col = lax.broadcasted_iota(jnp.int32, (B, 3 * B), 1)
    qp = n * B + row
    kp = jnp.where(col < B, col, jnp.where(col < 2 * B, (n - 1) * B + col - B, n * B + col - 2 * B))
    seg_ok = ((col < B) & (n >= 2) & (kp < n_meta)) | ((col >= B) & (col < 2 * B) & (n >= 1)) | (col >= 2 * B)
    return seg_ok & (kp <= qp) & ((qp - kp < WINDOW) | (kp < n_meta))


_NEG = -1e30
_NT = (((1,), (1,)), ((), ()))
_TN = (((0,), (0,)), ((), ()))


def _softmax_sink(s, mask, sink):
    s = jnp.where(mask, s, _NEG)
    m = jnp.maximum(jnp.max(s, axis=1, keepdims=True), sink)
    p = jnp.exp(s - m)
    es = jnp.exp(sink - m)
    inv = 1.0 / (jnp.sum(p, axis=1, keepdims=True) + es)
    return p * inv, es * inv


def _attn_fwd(qkr, z, sinks, Q, KV, n_meta):
    T = qkr.shape[0]
    B = ATT_BLOCK
    NB = T // B
    NQ = Q // HEAD_DIM
    NKV = KV // HEAD_DIM
    G = NQ // NKV
    kcol, vcol = Q // KV, (Q + KV) // KV
    assert Q % KV == 0

    def body(q_ref, kM, kP, kC, vM, vP, vC, s_ref, o_ref):
        n = pl.program_id(0)
        mask = _attn_mask(n, n_meta)
        for h in range(NKV):
            hs = slice(h * HEAD_DIM, (h + 1) * HEAD_DIM)
            kh = jnp.concatenate([kM[:, hs], kP[:, hs], kC[:, hs]], axis=0)
            vh = jnp.concatenate([vM[:, hs], vP[:, hs], vC[:, hs]], axis=0)
            for g in range(G):
                hq = h * G + g
                qs = slice(hq * HEAD_DIM, (hq + 1) * HEAD_DIM)
                s = lax.dot_general(q_ref[:, qs], kh, _NT, preferred_element_type=F32)
                p, _ = _softmax_sink(s, mask, s_ref[0, hq])
                o = jnp.dot(p.astype(BF16), vh, preferred_element_type=F32)
                o_ref[:, qs] = o.astype(BF16)

    def kv_spec(which, colblk):
        if which == 0:
            return pl.BlockSpec((B, KV), lambda n: (0, colblk))
        if which == 1:
            return pl.BlockSpec((B, KV), lambda n: (jnp.maximum(n - 1, 0), colblk))
        return pl.BlockSpec((B, KV), lambda n: (n, colblk))

    return pl.pallas_call(
        body, name="attn_fwd", grid=(NB,),
        in_specs=[pl.BlockSpec((B, Q), lambda n: (n, 0)),
                  kv_spec(0, kcol), kv_spec(1, kcol), kv_spec(2, kcol),
                  kv_spec(0, vcol), kv_spec(1, vcol), kv_spec(2, vcol),
                  pl.BlockSpec(memory_space=pltpu.SMEM)],
        out_specs=pl.BlockSpec((B, Q), lambda n: (n, 0)),
        out_shape=jax.ShapeDtypeStruct((T, Q), BF16), compiler_params=_params(("parallel",)),
    )(qkr, qkr, qkr, qkr, z, z, z, sinks)


def _attn_bwd(qkr, z, sinks, dao, Q, KV, n_meta):
    T = qkr.shape[0]
    B = ATT_BLOCK
    NB = T // B
    NQ = Q // HEAD_DIM
    NKV = KV // HEAD_DIM
    G = NQ // NKV
    kcol, vcol = Q // KV, (Q + KV) // KV
    assert NQ <= LANES

    def body(q_ref, do_ref, kM, kP, kC, vM, vP, vC, s_ref,
             dq_ref, dk_ref, dv_ref, dkm_ref, dvm_ref, ds_ref, ck_ref, cv_ref):
        n = pl.program_id(0)

        @pl.when(n == 0)
        def _():
            dkm_ref[...] = jnp.zeros_like(dkm_ref)
            dvm_ref[...] = jnp.zeros_like(dvm_ref)
            ds_ref[...] = jnp.zeros_like(ds_ref)
            ck_ref[...] = jnp.zeros_like(ck_ref)
            cv_ref[...] = jnp.zeros_like(cv_ref)

        @pl.when(n < NB)
        def _():
            mask = _attn_mask(n, n_meta)
            lane = lax.broadcasted_iota(jnp.int32, (1, LANES), 1)
            dsink = jnp.zeros((1, LANES), F32)
            for h in range(NKV):
                hs = slice(h * HEAD_DIM, (h + 1) * HEAD_DIM)
                kh = jnp.concatenate([kM[:, hs], kP[:, hs], kC[:, hs]], axis=0)
                vh = jnp.concatenate([vM[:, hs], vP[:, hs], vC[:, hs]], axis=0)
                dkh = jnp.zeros((3 * B, HEAD_DIM), F32)
                dvh = jnp.zeros((3 * B, HEAD_DIM), F32)
                for g in range(G):
                    hq = h * G + g
                    qs = slice(hq * HEAD_DIM, (hq + 1) * HEAD_DIM)
                    q = q_ref[:, qs]
                    do = do_ref[:, qs]
                    s = lax.dot_general(q, kh, _NT, preferred_element_type=F32)
                    p, psink = _softmax_sink(s, mask, s_ref[0, hq])
                    dp = lax.dot_general(do, vh, _NT, preferred_element_type=F32)
                    delta = jnp.sum(p * dp, axis=1, keepdims=True)
                    dsc = (p * (dp - delta)).astype(BF16)
                    dsink = dsink + jnp.where(lane == hq, -jnp.sum(psink * delta), 0.0)
                    dq_ref[:, qs] = jnp.dot(dsc, kh, preferred_element_type=F32)
                    dkh = dkh + lax.dot_general(dsc, q, _TN, preferred_element_type=F32)
                    dvh = dvh + lax.dot_general(p.astype(BF16), do, _TN, preferred_element_type=F32)
                dkm_ref[:, hs] += dkh[0:B]
                dvm_ref[:, hs] += dvh[0:B]
                dk_ref[:, hs] = ck_ref[:, hs] + dkh[B:2 * B]
                dv_ref[:, hs] = cv_ref[:, hs] + dvh[B:2 * B]
                ck_ref[:, hs] = dkh[2 * B:3 * B]
                cv_ref[:, hs] = dvh[2 * B:3 * B]
            ds_ref[...] += dsink

        @pl.when(n == NB)
        def _():
            dk_ref[...] = ck_ref[...]
            dv_ref[...] = cv_ref[...]

    last = NB - 1

    def kv_spec(which, colblk):
        if which == 0:
            return pl.BlockSpec((B, KV), lambda n: (0, colblk))
        if which == 1:
            return pl.BlockSpec((B, KV), lambda n: (jnp.clip(n - 1, 0, last), colblk))
        return pl.BlockSpec((B, KV), lambda n: (jnp.minimum(n, last), colblk))

    qspec = pl.BlockSpec((B, Q), lambda n: (jnp.minimum(n, last), 0))
    carry = pl.BlockSpec((B, KV), lambda n: (jnp.maximum(n - 1, 0), 0))
    const = pl.BlockSpec((B, KV), lambda n: (0, 0))
    return pl.pallas_call(
        body, name="attn_bwd", grid=(NB + 1,),
        in_specs=[qspec, qspec,
                  kv_spec(0, kcol), kv_spec(1, kcol), kv_spec(2, kcol),
                  kv_spec(0, vcol), kv_spec(1, vcol), kv_spec(2, vcol),
                  pl.BlockSpec(memory_space=pltpu.SMEM)],
        out_specs=[qspec, carry, carry, const, const, pl.BlockSpec((1, LANES), lambda n: (0, 0))],
        out_shape=[jax.ShapeDtypeStruct((T, Q), F32), jax.ShapeDtypeStruct((T, KV), F32),
                   jax.ShapeDtypeStruct((T, KV), F32), jax.ShapeDtypeStruct((B, KV), F32),
                   jax.ShapeDtypeStruct((B, KV), F32), jax.ShapeDtypeStruct((1, LANES), F32)],
        scratch_shapes=[pltpu.VMEM((B, KV), F32), pltpu.VMEM((B, KV), F32)],
        compiler_params=_params(("arbitrary",)),
    )(qkr, dao, qkr, qkr, qkr, z, z, z, sinks)


def _rope_bwd(dz, dq, dk, dv, dkm, dvm, cosf, sinf, Q, KV):
    T = dq.shape[0]
    B = ATT_BLOCK
    W = Q + 2 * KV

    def body(dq_ref, dk_ref, dv_ref, dkm_ref, dvm_ref, c_ref, s_ref, dz_in, o_ref):
        first = (pl.program_id(0) == 0).astype(F32)
        c = c_ref[...]
        s = s_ref[...]

        def unrope(d):
            reps = d.shape[1] // LANES
            return d * jnp.tile(c, (1, reps)) - _swap_halves(d) * jnp.tile(s, (1, reps))

        o_ref[:, 0:Q] = (unrope(dq_ref[...]) * (HEAD_DIM ** -0.5)).astype(BF16)
        o_ref[:, Q:Q + KV] = unrope(dk_ref[...] + first * dkm_ref[...]).astype(BF16)
        o_ref[:, Q + KV:W] = (dv_ref[...] + first * dvm_ref[...]).astype(BF16)

    kvs = pl.BlockSpec((B, KV), lambda i: (i, 0))
    const = pl.BlockSpec((B, KV), lambda i: (0, 0))
    tab = pl.BlockSpec((B, LANES), lambda i: (i, 0))
    return pl.pallas_call(
        body, name="rope_bwd", grid=(T // B,),
        in_specs=[pl.BlockSpec((B, Q), lambda i: (i, 0)), kvs, kvs, const, const, tab, tab, ANY],
        out_specs=pl.BlockSpec((B, W), lambda i: (i, 0)),
        out_shape=jax.ShapeDtypeStruct(dz.shape, dz.dtype), input_output_aliases={7: 0},
        compiler_params=_params(("parallel",)),
    )(dq, dk, dv, dkm, dvm, cosf, sinf, dz)


def _glu_rows(a_ref, g_ref):
    return a_ref[...].astype(F32) * jax.nn.sigmoid(g_ref[...].astype(F32))


def _conv_fwd(z, conv_w, conv_b, a_col, CV):
    T = z.shape[0]
    cw = _tile(CV, 512)
    tr = _tile(T, 384, CONV_HALO)
    H = CONV_HALO
    ab, gb = a_col // cw, (a_col + CV) // cw
    assert a_col % cw == 0 and CV % cw == 0
    hpt = tr // H

    def body(a_ref, g_ref, ap_ref, gp_ref, w_ref, b_ref, co_ref, cbuf):
        has_prev = (pl.program_id(0) > 0).astype(F32)
        cbuf[0:H, :] = _glu_rows(ap_ref, gp_ref) * has_prev
        cbuf[H:H + tr, :] = _glu_rows(a_ref, g_ref)
        acc = jnp.broadcast_to(b_ref[...], (tr, cw))
        for j in range(CONV_WIDTH):
            acc = acc + w_ref[j:j + 1, :] * cbuf[pl.ds(H - (CONV_WIDTH - 1) + j, tr), :]
        co_ref[...] = acc

    cur = lambda blk: pl.BlockSpec((tr, cw), lambda i, j: (i, blk + j))
    prev = lambda blk: pl.BlockSpec((H, cw), lambda i, j: (jnp.maximum(i * hpt - 1, 0), blk + j))
    return pl.pallas_call(
        body, name="conv_fwd", grid=(T // tr, CV // cw),
        in_specs=[cur(ab), cur(gb), prev(ab), prev(gb),
                  pl.BlockSpec((32, cw), lambda i, j: (0, j)), pl.BlockSpec((1, cw), lambda i, j: (0, j))],
        out_specs=pl.BlockSpec((tr, cw), lambda i, j: (i, j)),
        out_shape=jax.ShapeDtypeStruct((T, CV), F32),
        scratch_shapes=[pltpu.VMEM((tr + H, cw), F32)],
        compiler_params=_params(("parallel", "parallel")),
    )(z, z, z, z, conv_w, conv_b)


def _ln_parts(x, g, b):
    mu = jnp.mean(x, axis=-1, keepdims=True)
    xc = x - mu
    r = lax.rsqrt(jnp.mean(xc * xc, axis=-1, keepdims=True) + EPS)
    xh = xc * r
    return xh, r, xh * g + b


def _ln_swish_fwd(co, g, b):
    T, CV = co.shape
    tr = _tile(T, 128, 16)

    def body(x_ref, g_ref, b_ref, o_ref):
        _, _, y = _ln_parts(x_ref[...], g_ref[...], b_ref[...])
        o_ref[...] = (y * jax.nn.sigmoid(y)).astype(BF16)

    row = pl.BlockSpec((tr, CV), lambda i: (i, 0))
    vec = pl.BlockSpec((1, CV), lambda i: (0, 0))
    return pl.pallas_call(
        body, name="ln_swish_fwd", grid=(T // tr,), in_specs=[row, vec, vec], out_specs=row,
        out_shape=jax.ShapeDtypeStruct((T, CV), BF16), compiler_params=_params(("parallel",)),
    )(co, g, b)


def _ln_swish_bwd(dcs, co, g, b):
    T, CV = co.shape
    tr = _tile(T, 128, 16)

    def body(d_ref, x_ref, g_ref, b_ref, dx_ref, dg_ref, db_ref):
        gain = g_ref[...]
        xh, r, y = _ln_parts(x_ref[...], gain, b_ref[...])
        sg = jax.nn.sigmoid(y)
        dy = d_ref[...].astype(F32) * (sg * (1.0 + y * (1.0 - sg)))
        dxh = dy * gain
        dx_ref[...] = r * (dxh - jnp.mean(dxh, axis=-1, keepdims=True)
                           - xh * jnp.mean(dxh * xh, axis=-1, keepdims=True))
        gpart = jnp.sum(dy * xh, axis=0, keepdims=True)
        bpart = jnp.sum(dy, axis=0, keepdims=True)

        @pl.when(pl.program_id(0) == 0)
        def _():
            dg_ref[...] = gpart
            db_ref[...] = bpart

        @pl.when(pl.program_id(0) > 0)
        def _():
            dg_ref[...] += gpart
            db_ref[...] += bpart

    row = pl.BlockSpec((tr, CV), lambda i: (i, 0))
    vec = pl.BlockSpec((1, CV), lambda i: (0, 0))
    return pl.pallas_call(
        body, name="ln_swish_bwd", grid=(T // tr,), in_specs=[row, row, vec, vec], out_specs=[row, vec, vec],
        out_shape=[jax.ShapeDtypeStruct((T, CV), F32), jax.ShapeDtypeStruct((1, CV), F32),
                   jax.ShapeDtypeStruct((1, CV), F32)],
        compiler_params=_params(("arbitrary",)),
    )(dcs, co, g, b)


def _conv_bwd(dco, z, conv_w, a_col, CV):
    T = z.shape[0]
    cw = _tile(CV, 512)
    tr = _tile(T, 384, CONV_HALO)
    H = CONV_HALO
    ab, gb = a_col // cw, (a_col + CV) // cw
    hpt = tr // H
    ni = T // tr
    last_halo = T // H - 1

    def body(d_ref, dn_ref, a_ref, g_ref, ap_ref, gp_ref, w_ref, dc_ref, dw_ref, db_ref, cbuf, dbuf):
        i = pl.program_id(1)
        has_prev = (i > 0).astype(F32)
        has_next = (i < ni - 1).astype(F32)
        d = d_ref[...]
        cbuf[0:H, :] = _glu_rows(ap_ref, gp_ref) * has_prev
        cbuf[H:H + tr, :] = _glu_rows(a_ref, g_ref)
        dbuf[0:tr, :] = d
        dbuf[tr:tr + H, :] = dn_ref[...] * has_next

        @pl.when(i == 0)
        def _():
            dw_ref[...] = jnp.zeros_like(dw_ref)
            db_ref[...] = jnp.zeros_like(db_ref)

        acc = jnp.zeros((tr, cw), F32)
        for j in range(CONV_WIDTH):
            acc = acc + w_ref[j:j + 1, :] * dbuf[pl.ds(CONV_WIDTH - 1 - j, tr), :]
            dw_ref[j:j + 1, :] += jnp.sum(d * cbuf[pl.ds(H - (CONV_WIDTH - 1) + j, tr), :], axis=0, keepdims=True)
        dc_ref[...] = acc
        db_ref[...] += jnp.sum(d, axis=0, keepdims=True)

    cur = lambda blk: pl.BlockSpec((tr, cw), lambda j, i: (i, blk + j))
    prev = lambda blk: pl.BlockSpec((H, cw), lambda j, i: (jnp.maximum(i * hpt - 1, 0), blk + j))
    nxt = pl.BlockSpec((H, cw), lambda j, i: (jnp.minimum((i + 1) * hpt, last_halo), j))
    return pl.pallas_call(
        body, name="conv_bwd", grid=(CV // cw, ni),
        in_specs=[cur(0), nxt, cur(ab), cur(gb), prev(ab), prev(gb), pl.BlockSpec((32, cw), lambda j, i: (0, j))],
        out_specs=[pl.BlockSpec((tr, cw), lambda j, i: (i, j)), pl.BlockSpec((32, cw), lambda j, i: (0, j)),
                   pl.BlockSpec((1, cw), lambda j, i: (0, j))],
        out_shape=[jax.ShapeDtypeStruct((T, CV), F32), jax.ShapeDtypeStruct((32, CV), F32),
                   jax.ShapeDtypeStruct((1, CV), F32)],
        scratch_shapes=[pltpu.VMEM((tr + H, cw), F32), pltpu.VMEM((tr + H, cw), F32)],
        compiler_params=_params(("parallel", "arbitrary")),
    )(dco, dco, z, z, z, z, conv_w)


def _glu_bwd(dz, dc, z, a_col, CV):
    T = z.shape[0]
    cw = _tile(CV, 512)
    tr = _tile(T, 384, 16)
    nc = CV // cw
    ab = a_col // cw

    def body(dc_ref, a_ref, g_ref, dz_in, o_ref):
        j = pl.program_id(1)
        d = dc_ref[...]
        sg = jax.nn.sigmoid(g_ref[...].astype(F32))

        @pl.when(j < nc)
        def _():
            o_ref[...] = (d * sg).astype(BF16)

        @pl.when(j >= nc)
        def _():
            o_ref[...] = (d * a_ref[...].astype(F32) * sg * (1.0 - sg)).astype(BF16)

    return pl.pallas_call(
        body, name="glu_bwd", grid=(T // tr, 2 * nc),
        in_specs=[pl.BlockSpec((tr, cw), lambda i, j: (i, j % nc)),
                  pl.BlockSpec((tr, cw), lambda i, j: (i, ab + j % nc)),
                  pl.BlockSpec((tr, cw), lambda i, j: (i, ab + nc + j % nc)), ANY],
        out_specs=pl.BlockSpec((tr, cw), lambda i, j: (i, ab + j)),
        out_shape=jax.ShapeDtypeStruct(dz.shape, dz.dtype), input_output_aliases={3: 0},
        compiler_params=_params(("parallel", "parallel")),
    )(dc, z, z, dz)


def _gate_fwd(z, ab, ga_col, D):
    T = z.shape[0]
    cw = _tile(D, 512)
    tr = _tile(T, 384, 16)
    nd = D // cw
    gblk = ga_col // cw
    assert ga_col % cw == 0

    def body(ga_ref, gb_ref, a_ref, b_ref, o_ref):
        m = (jax.nn.sigmoid(ga_ref[...].astype(F32)) * a_ref[...].astype(F32)
             + jax.nn.sigmoid(gb_ref[...].astype(F32)) * b_ref[...].astype(F32))
        o_ref[...] = m.astype(BF16)

    blk = lambda off: pl.BlockSpec((tr, cw), lambda i, j: (i, off + j))
    return pl.pallas_call(
        body, name="gate_fwd", grid=(T // tr, nd),
        in_specs=[blk(gblk), blk(gblk + nd), blk(0), blk(nd)], out_specs=blk(0),
        out_shape=jax.ShapeDtypeStruct((T, D), BF16), compiler_params=_params(("parallel", "parallel")),
    )(z, z, ab, ab)


def _gate_bwd(dz, dmg, z, ab, ga_col, D):
    T = z.shape[0]
    cw = _tile(D, 512)
    tr = _tile(T, 384, 16)
    nd = D // cw
    gblk = ga_col // cw

    def body(dm_ref, gt_ref, ab_ref, dz_in, dgt_ref, dab_ref):
        dm = dm_ref[...].astype(F32)
        sg = jax.nn.sigmoid(gt_ref[...].astype(F32))
        dab_ref[...] = (dm * sg).astype(BF16)
        dgt_ref[...] = (dm * ab_ref[...].astype(F32) * sg * (1.0 - sg)).astype(BF16)

    return pl.pallas_call(
        body, name="gate_bwd", grid=(T // tr, 2 * nd),
        in_specs=[pl.BlockSpec((tr, cw), lambda i, j: (i, j % nd)),
                  pl.BlockSpec((tr, cw), lambda i, j: (i, gblk + j)),
                  pl.BlockSpec((tr, cw), lambda i, j: (i, j)), ANY],
        out_specs=[pl.BlockSpec((tr, cw), lambda i, j: (i, gblk + j)), pl.BlockSpec((tr, cw), lambda i, j: (i, j))],
        out_shape=[jax.ShapeDtypeStruct(dz.shape, dz.dtype), jax.ShapeDtypeStruct((T, 2 * D), BF16)],
        input_output_aliases={3: 0}, compiler_params=_params(("parallel", "parallel")),
    )(dmg, z, ab, dz)


def _swiglu_fwd(gu, F):
    T = gu.shape[0]
    cw = _tile(F, 512)
    tr = _tile(T, 384, 16)
    nf = F // cw

    def body(g_ref, u_ref, o_ref):
        g = g_ref[...].astype(F32)
        o_ref[...] = (g * jax.nn.sigmoid(g) * u_ref[...].astype(F32)).astype(BF16)

    blk = lambda off: pl.BlockSpec((tr, cw), lambda i, j: (i, off + j))
    return pl.pallas_call(
        body, name="swiglu_fwd", grid=(T // tr, nf), in_specs=[blk(0), blk(nf)], out_specs=blk(0),
        out_shape=jax.ShapeDtypeStruct((T, F), BF16), compiler_params=_params(("parallel", "parallel")),
    )(gu, gu)


def _swiglu_bwd(dact, gu, F):
    T = gu.shape[0]
    cw = _tile(F, 512)
    tr = _tile(T, 384, 16)
    nf = F // cw

    def body(d_ref, g_ref, u_ref, o_ref):
        j = pl.program_id(1)
        d = d_ref[...].astype(F32)
        g = g_ref[...].astype(F32)
        sg = jax.nn.sigmoid(g)

        @pl.when(j < nf)
        def _():
            o_ref[...] = (d * u_ref[...].astype(F32) * sg * (1.0 + g * (1.0 - sg))).astype(BF16)

        @pl.when(j >= nf)
        def _():
            o_ref[...] = (d * g * sg).astype(BF16)

    half = lambda off: pl.BlockSpec((tr, cw), lambda i, j: (i, off + j % nf))
    return pl.pallas_call(
        body, name="swiglu_bwd", grid=(T // tr, 2 * nf), in_specs=[half(0), half(0), half(nf)],
        out_specs=pl.BlockSpec((tr, cw), lambda i, j: (i, j)),
        out_shape=jax.ShapeDtypeStruct((T, 2 * F), BF16), compiler_params=_params(("parallel", "parallel")),
    )(dact, gu, gu)


def _coords():
    return lax.axis_index("x"), lax.axis_index("y"), lax.axis_index("c")


def _flip(v, bit):
    return 1 - v if bit else v


def _chip_peer(k, x, y):
    px, py = _flip(x, (k >> 1) & 1), _flip(y, k & 1)
    return px, py, 2 * px + py


def _remote(src, dst, send_sem, recv_sem, device):
    return pltpu.make_async_remote_copy(src_ref=src, dst_ref=dst, send_sem=send_sem, recv_sem=recv_sem,
                                        device_id=device, device_id_type=MESH)


def _allgather_sum_small(buf):
    R = buf.shape[0]

    def body(x_ref, all_ref, sum_ref, send_sems, recv_sems, local_sem):
        x, y, c = _coords()
        me = 4 * x + 2 * y + c
        mine = pltpu.make_async_copy(x_ref, all_ref.at[me], local_sem)
        mine.start()
        sends = []
        for k in range(1, N_DEV):
            peer = (_flip(x, (k >> 2) & 1), _flip(y, (k >> 1) & 1), _flip(c, k & 1))
            cp = _remote(x_ref, all_ref.at[me], send_sems.at[k - 1], recv_sems.at[k - 1], peer)
            cp.start()
            sends.append(cp)
        for k in range(1, N_DEV):
            sender = 4 * _flip(x, (k >> 2) & 1) + 2 * _flip(y, (k >> 1) & 1) + _flip(c, k & 1)
            _remote(x_ref, all_ref.at[sender], send_sems.at[k - 1], recv_sems.at[k - 1], (x, y, c)).wait_recv()
        for cp in sends:
            cp.wait_send()
        mine.wait()
        acc = all_ref[0]
        for d in range(1, N_DEV):
            acc = acc + all_ref[d]
        sum_ref[...] = acc

    vm = pl.BlockSpec(memory_space=pltpu.VMEM)
    return pl.pallas_call(
        body, name="allgather_sum_small", in_specs=[vm], out_specs=[vm, vm],
        out_shape=[jax.ShapeDtypeStruct((N_DEV, R, LANES), F32), jax.ShapeDtypeStruct((R, LANES), F32)],
        scratch_shapes=[pltpu.SemaphoreType.DMA((N_DEV - 1,)), pltpu.SemaphoreType.DMA((N_DEV - 1,)),
                        pltpu.SemaphoreType.DMA],
        compiler_params=pltpu.CompilerParams(vmem_limit_bytes=VMEM_LIMIT),
    )(buf)


def _all_gather_weights(shards, axes):
    nw = len(shards)
    out_shapes = []
    for sh, ax in zip(shards, axes):
        rs, ns = sh.shape
        out_shapes.append(jax.ShapeDtypeStruct((rs, 4 * ns) if ax == 1 else (4 * rs, ns), sh.dtype))

    def body(*refs):
        ins, outs = refs[:nw], refs[nw:2 * nw]
        local_sems, send_sems, recv_sems = refs[2 * nw:]
        x, y, c = _coords()
        s = 2 * x + y

        def slot(w, shard, half):
            rs, ns = shards[w].shape
            hr = rs // 2
            if axes[w] == 1:
                rows = pl.ds(0, rs) if half is None else pl.ds(pl.multiple_of(half * hr, 16), hr)
                return outs[w].at[rows, pl.ds(pl.multiple_of(shard * ns, LANES), ns)]
            r0 = shard * rs if half is None else shard * rs + half * hr
            return outs[w].at[pl.ds(pl.multiple_of(r0, 16), rs if half is None else hr), :]

        def my_half(w):
            hr = shards[w].shape[0] // 2
            return ins[w].at[pl.ds(pl.multiple_of(c * hr, 16), hr), :]

        locals_ = [pltpu.make_async_copy(ins[w], slot(w, s, None), local_sems.at[w]) for w in range(nw)]
        for cp in locals_:
            cp.start()
        sends = []
        for w in range(nw):
            for k in (1, 2, 3):
                px, py, _ = _chip_peer(k, x, y)
                cp = _remote(my_half(w), slot(w, s, c), send_sems.at[6 * w + k - 1], recv_sems.at[6 * w + k - 1],
                             (px, py, c))
                cp.start()
                sends.append(cp)
        for w in range(nw):
            for k in (1, 2, 3):
                _, _, ps = _chip_peer(k, x, y)
                landed = slot(w, ps, c)
                _remote(landed, landed, send_sems.at[6 * w + k - 1], recv_sems.at[6 * w + k - 1], (x, y, c)).wait_recv()
                cp = _remote(landed, landed, send_sems.at[6 * w + 2 + k], recv_sems.at[6 * w + 2 + k], (x, y, 1 - c))
                cp.start()
                sends.append(cp)
        for w in range(nw):
            for k in (1, 2, 3):
                _, _, ps = _chip_peer(k, x, y)
                relayed = slot(w, ps, 1 - c)
                _remote(relayed, relayed, send_sems.at[6 * w + 2 + k], recv_sems.at[6 * w + 2 + k], (x, y, c)).wait_recv()
        for cp in sends:
            cp.wait_send()
        for cp in locals_:
            cp.wait()

    return pl.pallas_call(
        body, name="all_gather_weights", in_specs=[ANY] * nw, out_specs=[ANY] * nw, out_shape=out_shapes,
        scratch_shapes=[pltpu.SemaphoreType.DMA((nw,)), pltpu.SemaphoreType.DMA((6 * nw,)),
                        pltpu.SemaphoreType.DMA((6 * nw,))],
    )(*shards)


def _half_view(ref, axis, half):
    r, n = ref.shape
    if axis == 1:
        return ref.at[pl.ds(pl.multiple_of(half * (r // 2), 16), r // 2), :]
    return ref.at[:, pl.ds(pl.multiple_of(half * (n // 2), LANES), n // 2)]


def _half_shape(shape, axis):
    r, n = shape
    return (r // 2, n) if axis == 1 else (r, n // 2)


def _shard_shape(shape, axis):
    r, n = shape
    return (r, n // 4) if axis == 1 else (r // 4, n)


def _shard_view(ref, axis, shard):
    r, n = ref.shape
    if axis == 1:
        return ref.at[:, pl.ds(pl.multiple_of(shard * (n // 4), LANES), n // 4)]
    return ref.at[pl.ds(pl.multiple_of(shard * (r // 4), 16), r // 4), :]


def _rs_pair_send(dws, axes):
    nw = len(dws)

    def body(*refs):
        ins, outs = refs[:nw], refs[nw:2 * nw]
        send_sems, recv_sems = refs[2 * nw:]
        x, y, c = _coords()
        cps = [_remote(_half_view(ins[w], axes[w], 1 - c), outs[w], send_sems.at[w], recv_sems.at[w], (x, y, 1 - c))
               for w in range(nw)]
        for cp in cps:
            cp.start()
        for cp in cps:
            cp.wait()

    return pl.pallas_call(
        body, name="rs_pair_send", in_specs=[ANY] * nw, out_specs=[ANY] * nw,
        out_shape=[jax.ShapeDtypeStruct(_half_shape(d.shape, a), d.dtype) for d, a in zip(dws, axes)],
        scratch_shapes=[pltpu.SemaphoreType.DMA((nw,)), pltpu.SemaphoreType.DMA((nw,))],
    )(*dws)


def _rs_chip_send(ps, axes):
    nw = len(ps)

    def body(*refs):
        ins, outs = refs[:nw], refs[nw:2 * nw]
        send_sems, recv_sems = refs[2 * nw:]
        x, y, c = _coords()
        cps = []
        for w in range(nw):
            for k in (1, 2, 3):
                px, py, pshard = _chip_peer(k, x, y)
                cps.append(_remote(_shard_view(ins[w], axes[w], pshard), outs[w].at[k - 1],
                                   send_sems.at[3 * w + k - 1], recv_sems.at[3 * w + k - 1], (px, py, c)))
        for cp in cps:
            cp.start()
        for cp in cps:
            cp.wait()

    return pl.pallas_call(
        body, name="rs_chip_send", in_specs=[ANY] * nw, out_specs=[ANY] * nw,
        out_shape=[jax.ShapeDtypeStruct((3,) + _shard_shape(p.shape, a), p.dtype) for p, a in zip(ps, axes)],
        scratch_shapes=[pltpu.SemaphoreType.DMA((3 * nw,)), pltpu.SemaphoreType.DMA((3 * nw,))],
    )(*ps)


def _rs_pair_swap(g2s):
    nw = len(g2s)

    def body(*refs):
        ins, outs = refs[:nw], refs[nw:2 * nw]
        send_sems, recv_sems = refs[2 * nw:]
        x, y, c = _coords()
        cps = [_remote(outs[w].at[c], outs[w].at[c], send_sems.at[w], recv_sems.at[w], (x, y, 1 - c))
               for w in range(nw)]
        for cp in cps:
            cp.start()
        for w, cp in enumerate(cps):
            cp.wait_send()
            _remote(outs[w].at[1 - c], outs[w].at[1 - c], send_sems.at[w], recv_sems.at[w], (x, y, c)).wait_recv()

    return pl.pallas_call(
        body, name="rs_pair_swap", in_specs=[ANY] * nw, out_specs=[ANY] * nw,
        out_shape=[jax.ShapeDtypeStruct(g.shape, g.dtype) for g in g2s],
        input_output_aliases={w: w for w in range(nw)},
        scratch_shapes=[pltpu.SemaphoreType.DMA((nw,)), pltpu.SemaphoreType.DMA((nw,))],
    )(*g2s)


def _ew_tiles(shape):
    r, n = shape
    return _tile(r, 256, 16), _tile(n, 1024)


def _pair_add(name, dw, sib, axis, c1):
    hr, hn = sib.shape
    tr, tn = _ew_tiles((hr, hn))
    ni, nj = hr // tr, hn // tn
    if axis == 1:
        own = lambda i, j, c: (c[0] * ni + i, j)
    else:
        own = lambda i, j, c: (i, c[0] * nj + j)

    def body(c_ref, a_ref, b_ref, o_ref):
        o_ref[...] = (a_ref[...].astype(F32) + b_ref[...].astype(F32)).astype(BF16)

    blk = pl.BlockSpec((tr, tn), lambda i, j, c: (i, j))
    return pl.pallas_call(
        body, name=name,
        grid_spec=pltpu.PrefetchScalarGridSpec(
            num_scalar_prefetch=1, grid=(ni, nj), in_specs=[pl.BlockSpec((tr, tn), own), blk], out_specs=blk),
        out_shape=jax.ShapeDtypeStruct((hr, hn), BF16), compiler_params=_params(("parallel", "parallel")),
    )(c1, dw, sib)


def _shard_sum(name, p, arr, axis, sc):
    sr, sn = arr.shape[1:]
    tr, tn = _ew_tiles((sr, sn))
    ni, nj = sr // tr, sn // tn
    if axis == 1:
        own = lambda i, j, sc: (i, sc[0] * nj + j)
    else:
        own = lambda i, j, sc: (sc[0] * ni + i, j)

    def body(sc_ref, p_ref, a_ref, o_ref):
        o_ref[...] = ((p_ref[...].astype(F32) + a_ref[0].astype(F32)) + a_ref[1].astype(F32)) + a_ref[2].astype(F32)

    return pl.pallas_call(
        body, name=name,
        grid_spec=pltpu.PrefetchScalarGridSpec(
            num_scalar_prefetch=1, grid=(ni, nj),
            in_specs=[pl.BlockSpec((tr, tn), own), pl.BlockSpec((3, tr, tn), lambda i, j, sc: (0, i, j))],
            out_specs=pl.BlockSpec((None, tr, tn), lambda i, j, sc: (sc[1], i, j))),
        out_shape=jax.ShapeDtypeStruct((2, sr, sn), F32), compiler_params=_params(("parallel", "parallel")),
    )(sc, p, arr)


def _adamw_math(w, g, m, v):
    m = ADAM_B1 * m + (1.0 - ADAM_B1) * g
    v = ADAM_B2 * v + (1.0 - ADAM_B2) * (g * g)
    m_hat = m / (1.0 - ADAM_B1 ** ADAM_STEP)
    v_hat = v / (1.0 - ADAM_B2 ** ADAM_STEP)
    delta = -ADAM_LR * (m_hat / (jnp.sqrt(v_hat) + ADAM_EPS) + ADAM_WD * w)
    return delta, m, v


def _adamw_halves(name, g2, w, m, v, axis):
    sr, sn = g2.shape[1:]
    tr, tn = _ew_tiles((sr, sn))
    ni, nj = sr // tr, sn // tn
    if axis == 1:
        full = lambda h, i, j: (h * ni + i, j)
    else:
        full = lambda h, i, j: (i, h * nj + j)

    def body(g_ref, w_ref, m_ref, v_ref, go_ref, d_ref, mo_ref, vo_ref):
        g = g_ref[...]
        delta, m2, v2 = _adamw_math(w_ref[...], g, m_ref[...], v_ref[...])
        go_ref[...] = g
        d_ref[...] = delta
        mo_ref[...] = m2
        vo_ref[...] = v2

    fb = pl.BlockSpec((tr, tn), full)
    shp = jax.ShapeDtypeStruct(w.shape, F32)
    return pl.pallas_call(
        body, name=name, grid=(2, ni, nj),
        in_specs=[pl.BlockSpec((None, tr, tn), lambda h, i, j: (h, i, j)), fb, fb, fb], out_specs=[fb] * 4,
        out_shape=[shp] * 4, compiler_params=_params(("parallel", "parallel", "parallel")),
    )(g2, w, m, v)


def _adamw_flat(name, g, w, m, v):
    R = g.shape[0]
    tr = _tile(R, 512, 8)

    def body(g_ref, w_ref, m_ref, v_ref, d_ref, mo_ref, vo_ref):
        delta, m2, v2 = _adamw_math(w_ref[...], g_ref[...], m_ref[...], v_ref[...])
        d_ref[...] = delta
        mo_ref[...] = m2
        vo_ref[...] = v2

    blk = pl.BlockSpec((tr, LANES), lambda i: (i, 0))
    shp = jax.ShapeDtypeStruct((R, LANES), F32)
    return pl.pallas_call(
        body, name=name, grid=(R // tr,), in_specs=[blk] * 4, out_specs=[blk] * 3, out_shape=[shp] * 3,
        compiler_params=_params(("parallel",)),
    )(g, w, m, v)


def _pack(parts):
    flat = jnp.concatenate([p.astype(F32).reshape(-1) for p in parts])
    n = flat.shape[0]
    total = -(-n // (8 * LANES)) * (8 * LANES)
    return jnp.pad(flat, (0, total - n)).reshape(total // LANES, LANES)


def _unpack(buf, shapes):
    flat = buf.reshape(-1)
    out, pos = [], 0
    for shp in shapes:
        n = 1
        for d in shp:
            n *= d
        out.append(flat[pos:pos + n].reshape(shp))
        pos += n
    return out


_WEIGHTS = ("meta_tokens", "mix_norm_g", "w_in", "b_in", "attn_sinks", "conv_w", "conv_b", "conv_ln_g",
            "conv_ln_b", "w_attn_o", "w_conv_o", "b_conv_o", "w_out", "ffn_norm_g", "w_gate_up", "w_down",
            "final_norm_g")
_BIG = ("w_in", "w_attn_o", "w_conv_o", "w_out", "w_gate_up", "w_down")
_BIG_AXES = (1, 1, 1, 0, 1, 0)


def kernel(x, meta_tokens, mix_norm_g, w_in, b_in, attn_sinks, conv_w, conv_b, conv_ln_g, conv_ln_b, w_attn_o, w_conv_o, b_conv_o, w_out, ffn_norm_g, w_gate_up, w_down, final_norm_g, loss_target, m_meta_tokens, m_mix_norm_g, m_w_in, m_b_in, m_attn_sinks, m_conv_w, m_conv_b, m_conv_ln_g, m_conv_ln_b, m_w_attn_o, m_w_conv_o, m_b_conv_o, m_w_out, m_ffn_norm_g, m_w_gate_up, m_w_down, m_final_norm_g, v_meta_tokens, v_mix_norm_g, v_w_in, v_b_in, v_attn_sinks, v_conv_w, v_conv_b, v_conv_ln_g, v_conv_ln_b, v_w_attn_o, v_w_conv_o, v_b_conv_o, v_w_out, v_ffn_norm_g, v_w_gate_up, v_w_down, v_final_norm_g):
    W = dict(meta_tokens=meta_tokens, mix_norm_g=mix_norm_g, w_in=w_in, b_in=b_in, attn_sinks=attn_sinks,
             conv_w=conv_w, conv_b=conv_b, conv_ln_g=conv_ln_g, conv_ln_b=conv_ln_b, w_attn_o=w_attn_o,
             w_conv_o=w_conv_o, b_conv_o=b_conv_o, w_out=w_out, ffn_norm_g=ffn_norm_g, w_gate_up=w_gate_up,
             w_down=w_down, final_norm_g=final_norm_g)
    M1 = dict(meta_tokens=m_meta_tokens, mix_norm_g=m_mix_norm_g, w_in=m_w_in, b_in=m_b_in,
              attn_sinks=m_attn_sinks, conv_w=m_conv_w, conv_b=m_conv_b, conv_ln_g=m_conv_ln_g,
              conv_ln_b=m_conv_ln_b, w_attn_o=m_w_attn_o, w_conv_o=m_w_conv_o, b_conv_o=m_b_conv_o,
              w_out=m_w_out, ffn_norm_g=m_ffn_norm_g, w_gate_up=m_w_gate_up, w_down=m_w_down,
              final_norm_g=m_final_norm_g)
    M2 = dict(meta_tokens=v_meta_tokens, mix_norm_g=v_mix_norm_g, w_in=v_w_in, b_in=v_b_in,
              attn_sinks=v_attn_sinks, conv_w=v_conv_w, conv_b=v_conv_b, conv_ln_g=v_conv_ln_g,
              conv_ln_b=v_conv_ln_b, w_attn_o=v_w_attn_o, w_conv_o=v_w_conv_o, b_conv_o=v_b_conv_o,
              w_out=v_w_out, ffn_norm_g=v_ffn_norm_g, w_gate_up=v_w_gate_up, w_down=v_w_down,
              final_norm_g=v_final_norm_g)

    SEQ, D = x.shape[1], x.shape[2]
    NM, Dq = meta_tokens.shape
    Q, CV, IN = w_attn_o.shape[1], w_conv_o.shape[1], b_in.shape[1]
    F = 4 * w_down.shape[1]
    KV = (IN - Q - 2 * CV - 2 * D) // 2
    NQ = attn_sinks.shape[1]
    CVq = conv_w.shape[3]
    T = -(-(NM + SEQ) // ATT_BLOCK) * ATT_BLOCK
    a_col = Q + 2 * KV
    ga_col = a_col + 2 * CV
    assert Q == NQ * HEAD_DIM and Dq * 4 == D and CVq * 4 == CV

    cx, cy, cc = _coords()
    shard = 2 * cx + cy

    small_shapes = [(NM, Dq), (CONV_WIDTH, CVq)]
    gathered, _ = _allgather_sum_small(_pack([meta_tokens, conv_w.reshape(CONV_WIDTH, CVq)]))
    per_chip = [_unpack(gathered[2 * sh], small_shapes) for sh in range(4)]
    meta_full = jnp.concatenate([p[0] for p in per_chip], axis=1)
    taps = jnp.concatenate([p[1] for p in per_chip], axis=1)
    taps = jnp.concatenate([taps, jnp.zeros((32 - CONV_WIDTH, CV), F32)], axis=0)

    win, wao, wco, wout, wgu, wdn = _all_gather_weights([W[n][0].astype(BF16) for n in _BIG], _BIG_AXES)

    pad = T - NM - SEQ
    h0 = jnp.concatenate([meta_full, x[0], jnp.zeros((pad, D), F32)], axis=0)
    tgt = jnp.concatenate([jnp.zeros((NM, D), F32), loss_target[0], jnp.zeros((pad, D), F32)], axis=0)
    cosf, sinf = _rope_tables(T)

    u1 = _rms_fwd("rms_mix", h0, mix_norm_g)
    z = _mm("mm_in", u1, win, "nn", T, IN, D, bias=b_in)
    qkr = _rope_fwd(z, cosf, sinf, Q, KV)
    ao = _attn_fwd(qkr, z, attn_sinks, Q, KV, NM)
    co = _conv_fwd(z, taps, conv_b, a_col, CV)
    cs = _ln_swish_fwd(co, conv_ln_g, conv_ln_b)
    ab = _mm("mm_attn_o", ao, wao, "nn", T, D, Q, out=lax.empty((T, 2 * D), BF16))
    ab = _mm("mm_conv_o", cs, wco, "nn", T, D, CV, bias=b_conv_o, out=ab, out_off=(0, D))
    mg = _gate_fwd(z, ab, ga_col, D)
    h1 = _mm("mm_out", mg, wout, "nn", T, D, D, res=h0, out_dtype=F32)
    u2 = _rms_fwd("rms_ffn", h1, ffn_norm_g)
    gu = _mm("mm_gate_up", u2, wgu, "nn", T, 2 * F, D)
    act = _swiglu_fwd(gu, F)
    h2 = _mm("mm_down", act, wdn, "nn", T, D, F, res=h1, out_dtype=F32)
    loss11, dh2, dh2b, d_final_g = _final_loss(h2, final_norm_g.reshape(1, D), tgt, NM, NM + SEQ)

    dact = _mm("mm_d_act", dh2b, wdn, "nt", T, F, D)
    dw_down = _mm("mm_dw_down", act, dh2b, "tn", F, D, T)
    dgu = _swiglu_bwd(dact, gu, F)
    du2 = _mm("mm_d_u2", dgu, wgu, "nt", T, D, 2 * F, out_dtype=F32)
    dw_gate_up = _mm("mm_dw_gate_up", u2, dgu, "tn", D, 2 * F, T)
    dh1, dh1b, d_ffn_g = _rms_bwd("rms_ffn_bwd", du2, h1, ffn_norm_g, dh2)
    dmg = _mm("mm_d_merged", dh1b, wout, "nt", T, D, D)
    dw_out = _mm("mm_dw_out", mg, dh1b, "tn", D, D, T)
    dz, dab = _gate_bwd(lax.empty((T, IN), BF16), dmg, z, ab, ga_col, D)
    dao = _mm("mm_d_attn", dab, wao, "nt", T, Q, D)
    dw_attn_o = _mm("mm_dw_attn_o", ao, dab, "tn", Q, D, T)
    dcs = _mm("mm_d_conv", dab, wco, "nt", T, CV, D, a_off=(0, D))
    dw_conv_o = _mm("mm_dw_conv_o", cs, dab, "tn", CV, D, T, b_off=(0, D))
    d_b_conv_o = _colsum("colsum_b_conv_o", dab, width=D, col_off=D)
    dco, d_ln_g, d_ln_b = _ln_swish_bwd(dcs, co, conv_ln_g, conv_ln_b)
    dc, d_taps, d_conv_b = _conv_bwd(dco, z, taps, a_col, CV)
    dz = _glu_bwd(dz, dc, z, a_col, CV)
    dq, dk, dv, dkm, dvm, d_sinks = _attn_bwd(qkr, z, attn_sinks, dao, Q, KV, NM)
    dz = _rope_bwd(dz, dq, dk, dv, dkm, dvm, cosf, sinf, Q, KV)
    du1 = _mm("mm_d_u1", dz, win, "nt", T, D, IN, out_dtype=F32)
    dw_in = _mm("mm_dw_in", u1, dz, "tn", D, IN, T)
    d_b_in = _colsum("colsum_b_in", dz)
    dh0, _, d_mix_g = _rms_bwd("rms_mix_bwd", du1, h0, mix_norm_g, dh1)
    grad_x = dh0[NM:NM + SEQ][None]

    red_shapes = [(1, 1), (1, D), (1, IN), (1, NQ), (1, CV), (1, CV), (1, CV), (1, D), (1, D), (1, D),
                  (NM, D), (CONV_WIDTH, CV)]
    _, red = _allgather_sum_small(_pack([
        loss11, d_mix_g, d_b_in, d_sinks[:, :NQ], d_conv_b, d_ln_g, d_ln_b, d_b_conv_o, d_ffn_g, d_final_g,
        dh0[:NM], d_taps[:CONV_WIDTH]]))
    (loss, g_mix, g_b_in, g_sinks, g_conv_b, g_ln_g, g_ln_b, g_b_conv_o, g_ffn, g_final, g_meta,
     g_taps) = _unpack(red, red_shapes)
    G = dict(
        meta_tokens=lax.dynamic_slice(g_meta, (0, shard * Dq), (NM, Dq)),
        mix_norm_g=g_mix, b_in=g_b_in, attn_sinks=g_sinks,
        conv_w=lax.dynamic_slice(g_taps, (0, shard * CVq), (CONV_WIDTH, CVq)).reshape(conv_w.shape),
        conv_b=g_conv_b, conv_ln_g=g_ln_g, conv_ln_b=g_ln_b, b_conv_o=g_b_conv_o, ffn_norm_g=g_ffn,
        final_norm_g=g_final.reshape(final_norm_g.shape))

    dws = [dw_in, dw_attn_o, dw_conv_o, dw_out, dw_gate_up, dw_down]
    c1 = jnp.reshape(cc, (1,)).astype(jnp.int32)
    sc = jnp.stack([shard, cc]).astype(jnp.int32)
    sibs = _rs_pair_send(dws, _BIG_AXES)
    pairs = [_pair_add(f"pair_add_{n}", dws[i], sibs[i], _BIG_AXES[i], c1) for i, n in enumerate(_BIG)]
    arrived = _rs_chip_send(pairs, _BIG_AXES)
    halves = [_shard_sum(f"shard_sum_{n}", pairs[i], arrived[i], _BIG_AXES[i], sc) for i, n in enumerate(_BIG)]
    halves = _rs_pair_swap(halves)

    D_, NM_, NV_ = {}, {}, {}
    for i, n in enumerate(_BIG):
        g, d, m2, v2 = _adamw_halves(f"adamw_{n}", halves[i], W[n][0], M1[n][0], M2[n][0], _BIG_AXES[i])
        G[n], D_[n], NM_[n], NV_[n] = g[None], d[None], m2[None], v2[None]

    small = [n for n in _WEIGHTS if n not in _BIG]
    shapes = [W[n].shape for n in small]
    d_s, m_s, v_s = _adamw_flat("adamw_small", _pack([G[n] for n in small]), _pack([W[n] for n in small]),
                                _pack([M1[n] for n in small]), _pack([M2[n] for n in small]))
    for n, d, m2, v2 in zip(small, _unpack(d_s, shapes), _unpack(m_s, shapes), _unpack(v_s, shapes)):
        D_[n], NM_[n], NV_[n] = d, m2, v2

    return (loss.reshape(()), grad_x, *[G[n] for n in _WEIGHTS], *[D_[n] for n in _WEIGHTS],
            *[NM_[n] for n in _WEIGHTS], *[NV_[n] for n in _WEIGHTS])
```

```python
import jax
import jax.numpy as jnp
from jax import lax
from jax.experimental import pallas as pl
from jax.experimental.pallas import tpu as pltpu

F32 = jnp.float32
BF16 = jnp.bfloat16
MESH = pl.DeviceIdType.MESH
ANY = pl.BlockSpec(memory_space=pl.ANY)

V7X_VMEM_BYTES = 64 * 1024 * 1024
VMEM_LIMIT = V7X_VMEM_BYTES - 8 * 1024 * 1024
LANES = 128
ATT_BLOCK = 128
WINDOW = 128
HEAD_DIM = 64
ROPE_THETA = 10000.0
EPS = 1e-6
CONV_WIDTH = 31
CONV_HALO = 32
N_DEV = 8

ADAM_LR = 0.001
ADAM_B1 = 0.9
ADAM_B2 = 0.999
ADAM_EPS = 1e-08
ADAM_WD = 0.01
ADAM_STEP = 10


def _tile(n, target, mult=LANES):
    best = None
    for t in range(mult, min(n, target) + 1, mult):
        if n % t == 0:
            best = t
    return best if best is not None else n


def _params(sem=None):
    return pltpu.CompilerParams(dimension_semantics=sem, vmem_limit_bytes=VMEM_LIMIT)


def _divisors(n, mult):
    return [d for d in range(mult, n + 1, mult) if n % d == 0]


class _Side:
    def __init__(self, *ops):
        self.ops = [op for op in ops if op is not None]
        self.ins, self.outs, self.sems, self.alias, self._spans = [], [], [], {}, []
        for op in self.ops:
            i0, o0, s0 = len(self.ins), len(self.outs), len(self.sems)
            self.ins += list(op.ins)
            self.outs += list(op.outs)
            self.sems += list(op.sems)
            for a, b in op.alias.items():
                self.alias[i0 + a] = o0 + b
            self._spans.append((i0, len(op.ins), o0, len(op.outs), s0, len(op.sems)))
            op.side, op.o0 = self, o0
        self.results = None

    def _run(self, what, ins, outs, sems):
        for op, (i0, ni, o0, no, s0, ns) in zip(self.ops, self._spans):
            getattr(op, what)(ins[i0:i0 + ni], outs[o0:o0 + no], sems[s0:s0 + ns])

    def start(self, ins, outs, sems):
        self._run("start", ins, outs, sems)

    def finish(self, ins, outs, sems):
        self._run("finish", ins, outs, sems)


class _Op:
    alias = {}

    def result(self, i=0):
        return self.side.results[self.o0 + i]


def _hosted_call(body, *, name, grid, in_specs, out_specs, out_shape, args, scratch_shapes=(), aliases=None,
                 sem=None, side=None):
    multi = isinstance(out_shape, (list, tuple))
    oshapes = list(out_shape) if multi else [out_shape]
    ospecs = list(out_specs) if multi else [out_specs]
    n_in, n_out, n_scr = len(args), len(oshapes), len(scratch_shapes)
    if side is not None and not side.ops:
        side = None
    s_in = len(side.ins) if side else 0
    s_out = len(side.outs) if side else 0

    def full(*refs):
        ins, sins = refs[:n_in], refs[n_in:n_in + s_in]
        o0 = n_in + s_in
        outs, souts = refs[o0:o0 + n_out], refs[o0 + n_out:o0 + n_out + s_out]
        c0 = o0 + n_out + s_out
        scr, ssem = refs[c0:c0 + n_scr], refs[c0 + n_scr:]
        if side is None:
            body(*ins, *outs, *scr)
            return
        if not grid:
            side.start(sins, souts, ssem)
            body(*ins, *outs, *scr)
            side.finish(sins, souts, ssem)
            return
        first = last = None
        for axis, extent in enumerate(grid):
            pid = pl.program_id(axis)
            f, l = pid == 0, pid == extent - 1
            first = f if first is None else first & f
            last = l if last is None else last & l

        @pl.when(first)
        def _():
            side.start(sins, souts, ssem)

        body(*ins, *outs, *scr)

        @pl.when(last)
        def _():
            side.finish(sins, souts, ssem)

    all_alias = dict(aliases or {})
    if side:
        for a, b in side.alias.items():
            all_alias[n_in + a] = n_out + b
    res = pl.pallas_call(
        full, name=name, grid=grid,
        in_specs=[*in_specs, *([ANY] * s_in)], out_specs=[*ospecs, *([ANY] * s_out)],
        out_shape=[*oshapes, *(side.outs if side else [])],
        scratch_shapes=[*scratch_shapes, *(side.sems if side else [])],
        input_output_aliases=all_alias, compiler_params=_params(sem),
    )(*args, *(side.ins if side else []))
    if side:
        side.results = list(res[n_out:])
    main = list(res[:n_out])
    return main if multi else main[0]


V7X_MXU_FLOPS = 900e12
V7X_HBM_BYTES_PER_S = 3.0e12
GRID_STEP_S = 0.35e-6
MM_VMEM_BUDGET = 46 * 1024 * 1024


def _mm_tiles(mode, M, N, C, out_bytes, has_res):
    row_mult = 128 if mode == "tn" else 16
    tms = [d for d in _divisors(M, row_mult) if d >= min(M, 256)]
    tns = [d for d in _divisors(N, LANES) if min(N, 256) <= d <= 1024] or [N]
    tcs = [d for d in _divisors(C, 16 if mode == "tn" else LANES) if d >= min(C, 512)]
    best = None
    for tc in tcs:
        nk = C // tc
        eff = 0.96 if nk == 1 else (0.85 if tc >= 3072 else 0.78 if tc >= 2048 else 0.66 if tc >= 1024 else 0.5)
        for tm in tms:
            for tn in tns:
                vmem = (4 * (tm * tc + tc * tn) + tm * tn * 4 * (2 if nk > 1 else 1)
                        + 2 * tm * tn * out_bytes + (8 * tm * tn if has_res else 0))
                if vmem > MM_VMEM_BUDGET:
                    continue
                a_bytes, b_bytes = 2 * M * C, 2 * N * C
                fixed = M * N * (out_bytes + (4 if has_res else 0))
                if nk == 1:
                    ij, ji = a_bytes + b_bytes * (M // tm), b_bytes + a_bytes * (N // tn)
                    order, hbm = ("ij", ij) if ij <= ji else ("ji", ji)
                else:
                    order, hbm = "ij", a_bytes * (N // tn) + b_bytes * (M // tm)
                steps = (M // tm) * (N // tn) * nk
                t = max(2.0 * M * N * C / (V7X_MXU_FLOPS * eff), (hbm + fixed) / V7X_HBM_BYTES_PER_S)
                t += steps * GRID_STEP_S
                if best is None or t < best[0]:
                    best = (t, tm, tn, tc, order)
    assert best is not None, (mode, M, N, C)
    return best[1:]


_DIMS = {"nn": (((1,), (0,)), ((), ())), "nt": (((1,), (1,)), ((), ())), "tn": (((0,), (0,)), ((), ()))}


def _mm(name, a, b, mode, M, N, C, *, a_off=(0, 0), b_off=(0, 0), bias=None, res=None,
        out_dtype=BF16, out=None, out_off=(0, 0), side=None, tiles=None):
    tm, tn, tc, order = tiles or _mm_tiles(mode, M, N, C, jnp.dtype(out_dtype).itemsize, res is not None)
    nk = C // tc

    def grid_map(f):
        return f if order == "ij" else (lambda j, i, k: f(i, j, k))
    (ar, ac), (br, bc), (orow, ocol) = a_off, b_off, out_off
    if mode == "nn":
        a_blk, b_blk = (tm, tc), (tc, tn)
        a_map = lambda i, j, k: (ar // tm + i, ac // tc + k)
        b_map = lambda i, j, k: (br // tc + k, bc // tn + j)
        assert ar % tm == 0 and ac % tc == 0 and br % tc == 0 and bc % tn == 0
    elif mode == "nt":
        a_blk, b_blk = (tm, tc), (tn, tc)
        a_map = lambda i, j, k: (ar // tm + i, ac // tc + k)
        b_map = lambda i, j, k: (br // tn + j, bc // tc + k)
        assert ar % tm == 0 and ac % tc == 0 and br % tn == 0 and bc % tc == 0
    else:
        a_blk, b_blk = (tc, tm), (tc, tn)
        a_map = lambda i, j, k: (ar // tc + k, ac // tm + i)
        b_map = lambda i, j, k: (br // tc + k, bc // tn + j)
        assert ar % tc == 0 and ac % tm == 0 and br % tc == 0 and bc % tn == 0
    assert orow % tm == 0 and ocol % tn == 0
    dims = _DIMS[mode]
    has_bias, has_res, has_out = bias is not None, res is not None, out is not None

    def body(*refs):
        refs = list(refs)
        a_ref, b_ref = refs[0], refs[1]
        pos = 2
        bias_ref = res_ref = None
        if has_bias:
            bias_ref = refs[pos]
            pos += 1
        if has_res:
            res_ref = refs[pos]
            pos += 1
        if has_out:
            pos += 1
        o_ref = refs[pos]
        acc_ref = refs[pos + 1] if nk > 1 else None

        def finish(r):
            if has_bias:
                r = r + bias_ref[...]
            if has_res:
                r = r + res_ref[...]
            o_ref[...] = r.astype(out_dtype)

        part = lax.dot_general(a_ref[...], b_ref[...], dims, preferred_element_type=F32)
        if nk == 1:
            finish(part)
        else:
            k = pl.program_id(2)

            @pl.when(k == 0)
            def _():
                acc_ref[...] = part

            @pl.when(k > 0)
            def _():
                acc_ref[...] += part

            @pl.when(k == nk - 1)
            def _():
                finish(acc_ref[...])

    in_specs = [pl.BlockSpec(a_blk, grid_map(a_map)), pl.BlockSpec(b_blk, grid_map(b_map))]
    args = [a, b]
    if has_bias:
        in_specs.append(pl.BlockSpec((1, tn), grid_map(lambda i, j, k: (0, j))))
        args.append(bias)
    if has_res:
        in_specs.append(pl.BlockSpec((tm, tn), grid_map(lambda i, j, k: (i, j))))
        args.append(res)
    aliases = {}
    if has_out:
        in_specs.append(ANY)
        aliases = {len(args): 0}
        args.append(out)
        out_shape = jax.ShapeDtypeStruct(out.shape, out.dtype)
        assert out.dtype == out_dtype
    else:
        out_shape = jax.ShapeDtypeStruct((M, N), out_dtype)
    grid = (M // tm, N // tn, nk) if order == "ij" else (N // tn, M // tm, nk)
    return _hosted_call(
        body, name=name, grid=grid, in_specs=in_specs,
        out_specs=pl.BlockSpec((tm, tn), grid_map(lambda i, j, k: (orow // tm + i, ocol // tn + j))),
        out_shape=out_shape, args=args, aliases=aliases,
        scratch_shapes=[pltpu.VMEM((tm, tn), F32)] if nk > 1 else [],
        sem=("parallel", "parallel", "arbitrary"), side=side)


def _rms_fwd(name, h, g):
    T, D = h.shape
    tr = _tile(T, 384, 8)

    def body(h_ref, g_ref, u_ref):
        x = h_ref[...]
        r = lax.rsqrt(jnp.mean(x * x, axis=-1, keepdims=True) + EPS)
        u_ref[...] = (x * r * g_ref[...]).astype(BF16)

    return pl.pallas_call(
        body, name=name, grid=(T // tr,),
        in_specs=[pl.BlockSpec((tr, D), lambda i: (i, 0)), pl.BlockSpec((1, D), lambda i: (0, 0))],
        out_specs=pl.BlockSpec((tr, D), lambda i: (i, 0)),
        out_shape=jax.ShapeDtypeStruct((T, D), BF16), compiler_params=_params(("parallel",)),
    )(h, g)


def _rms_bwd(name, du, h, g, dres, side=None):
    T, D = h.shape
    tr = _tile(T, 128, 8)

    def body(du_ref, h_ref, g_ref, dres_ref, dh_ref, dhb_ref, dg_ref):
        x = h_ref[...]
        r = lax.rsqrt(jnp.mean(x * x, axis=-1, keepdims=True) + EPS)
        xh = x * r
        d = du_ref[...].astype(F32)
        dxh = d * g_ref[...]
        dx = r * (dxh - xh * jnp.mean(dxh * xh, axis=-1, keepdims=True))
        dh = dres_ref[...] + dx
        dh_ref[...] = dh
        dhb_ref[...] = dh.astype(BF16)
        part = jnp.sum(d * xh, axis=0, keepdims=True)

        @pl.when(pl.program_id(0) == 0)
        def _():
            dg_ref[...] = part

        @pl.when(pl.program_id(0) > 0)
        def _():
            dg_ref[...] += part

    row = pl.BlockSpec((tr, D), lambda i: (i, 0))
    vec = pl.BlockSpec((1, D), lambda i: (0, 0))
    return _hosted_call(
        body, name=name, grid=(T // tr,), in_specs=[row, row, vec, row], out_specs=[row, row, vec],
        out_shape=[jax.ShapeDtypeStruct((T, D), F32), jax.ShapeDtypeStruct((T, D), BF16),
                   jax.ShapeDtypeStruct((1, D), F32)],
        args=[du, h, g, dres], sem=("arbitrary",), side=side)


def _final_loss(h, g, tgt, row_lo, row_hi):
    T, D = h.shape
    tr = _tile(T, 128, 8)

    def body(h_ref, g_ref, t_ref, loss_ref, dh_ref, dhb_ref, dg_ref):
        i = pl.program_id(0)
        x = h_ref[...]
        r = lax.rsqrt(jnp.mean(x * x, axis=-1, keepdims=True) + EPS)
        xh = x * r
        gain = g_ref[...]
        rows = i * tr + lax.broadcasted_iota(jnp.int32, (tr, 1), 0)
        real = (rows >= row_lo) & (rows < row_hi)
        e = jnp.where(real, xh * gain - t_ref[...], 0.0)
        lpart = 0.5 * jnp.sum(jnp.mean(e * e, axis=-1, keepdims=True), axis=0, keepdims=True)
        dy = e * (1.0 / D)
        dxh = dy * gain
        dx = r * (dxh - xh * jnp.mean(dxh * xh, axis=-1, keepdims=True))
        dh_ref[...] = dx
        dhb_ref[...] = dx.astype(BF16)
        gpart = jnp.sum(dy * xh, axis=0, keepdims=True)

        @pl.when(i == 0)
        def _():
            dg_ref[...] = gpart
            loss_ref[...] = lpart

        @pl.when(i > 0)
        def _():
            dg_ref[...] += gpart
            loss_ref[...] += lpart

    row = pl.BlockSpec((tr, D), lambda i: (i, 0))
    vec = pl.BlockSpec((1, D), lambda i: (0, 0))
    one = pl.BlockSpec((1, 1), lambda i: (0, 0))
    return pl.pallas_call(
        body, name="final_loss", grid=(T // tr,), in_specs=[row, vec, row], out_specs=[one, row, row, vec],
        out_shape=[jax.ShapeDtypeStruct((1, 1), F32), jax.ShapeDtypeStruct((T, D), F32),
                   jax.ShapeDtypeStruct((T, D), BF16), jax.ShapeDtypeStruct((1, D), F32)],
        compiler_params=_params(("arbitrary",)),
    )(h, g, tgt)


def _colsum(name, a, width=None, col_off=0, side=None):
    T = a.shape[0]
    width = width or a.shape[1]
    cw = _tile(width, 512)
    tr = _tile(T, 384, 8)
    assert col_off % cw == 0

    def body(a_ref, o_ref):
        part = jnp.sum(a_ref[...].astype(F32), axis=0, keepdims=True)

        @pl.when(pl.program_id(1) == 0)
        def _():
            o_ref[...] = part

        @pl.when(pl.program_id(1) > 0)
        def _():
            o_ref[...] += part

    return _hosted_call(
        body, name=name, grid=(width // cw, T // tr),
        in_specs=[pl.BlockSpec((tr, cw), lambda j, i: (i, col_off // cw + j))],
        out_specs=pl.BlockSpec((1, cw), lambda j, i: (0, j)),
        out_shape=jax.ShapeDtypeStruct((1, width), F32), args=[a], sem=("parallel", "arbitrary"), side=side)


def _rope_tables(T):
    pos = jnp.arange(T, dtype=F32)
    inv = ROPE_THETA ** (-jnp.arange(0, HEAD_DIM, 2, dtype=F32) / HEAD_DIM)
    ang = pos[:, None] * inv[None, :]
    c, s = jnp.cos(ang), jnp.sin(ang)
    reps = LANES // HEAD_DIM
    return (jnp.tile(jnp.concatenate([c, c], axis=1), (1, reps)),
            jnp.tile(jnp.concatenate([-s, s], axis=1), (1, reps)))


def _swap_halves(x):
    w = x.shape[1]
    half = HEAD_DIM // 2
    lane = lax.broadcasted_iota(jnp.int32, x.shape, 1)
    first = (lane & (HEAD_DIM - 1)) < half
    return jnp.where(first, pltpu.roll(x, w - half, 1), pltpu.roll(x, half, 1))


def _rope_fwd(z, cosf, sinf, Q, KV):
    T = z.shape[0]
    W = Q + KV
    tr = _tile(T, 384, 16)
    reps = W // LANES

    def body(z_ref, c_ref, s_ref, o_ref):
        x = z_ref[...].astype(F32)
        c = jnp.tile(c_ref[...], (1, reps))
        s = jnp.tile(s_ref[...], (1, reps))
        y = x * c + _swap_halves(x) * s
        lane = lax.broadcasted_iota(jnp.int32, y.shape, 1)
        y = jnp.where(lane < Q, y * (HEAD_DIM ** -0.5), y)
        o_ref[...] = y.astype(BF16)

    tab = pl.BlockSpec((tr, LANES), lambda i: (i, 0))
    return pl.pallas_call(
        body, name="rope_fwd", grid=(T // tr,),
        in_specs=[pl.BlockSpec((tr, W), lambda i: (i, 0)), tab, tab],
        out_specs=pl.BlockSpec((tr, W), lambda i: (i, 0)),
        out_shape=jax.ShapeDtypeStruct((T, W), BF16), compiler_params=_params(("parallel",)),
    )(z, cosf, sinf)


def _attn_mask(n, n_meta):
    B = ATT_BLOCK
    row = lax.broadcasted_iota(jnp.int32, (B, 3 * B), 0)
    col = lax.broadcasted_iota(jnp.int32, (B, 3 * B), 1)
    qp = n * B + row
    kp = jnp.where(col < B, col, jnp.where(col < 2 * B, (n - 1) * B + col - B, n * B + col - 2 * B))
    seg_ok = ((col < B) & (n >= 2) & (kp < n_meta)) | ((col >= B) & (col < 2 * B) & (n >= 1)) | (col >= 2 * B)
    return seg_ok & (kp <= qp) & ((qp - kp < WINDOW) | (kp < n_meta))


_NEG = -1e30
_NT = (((1,), (1,)), ((), ()))
_TN = (((0,), (0,)), ((), ()))


def _softmax_sink(s, mask, sink):
    s = jnp.where(mask, s, _NEG)
    m = jnp.maximum(jnp.max(s, axis=1, keepdims=True), sink)
    p = jnp.exp(s - m)
    es = jnp.exp(sink - m)
    inv = 1.0 / (jnp.sum(p, axis=1, keepdims=True) + es)
    return p * inv, es * inv


def _attn_fwd(qkr, z, sinks, Q, KV, n_meta, side=None):
    T = qkr.shape[0]
    B = ATT_BLOCK
    NB = T // B
    NQ = Q // HEAD_DIM
    NKV = KV // HEAD_DIM
    G = NQ // NKV
    kcol, vcol = Q // KV, (Q + KV) // KV
    assert Q % KV == 0

    def body(q_ref, kM, kP, kC, vM, vP, vC, s_ref, o_ref):
        n = pl.program_id(0)
        mask = _attn_mask(n, n_meta)
        for h in range(NKV):
            hs = slice(h * HEAD_DIM, (h + 1) * HEAD_DIM)
            kh = jnp.concatenate([kM[:, hs], kP[:, hs], kC[:, hs]], axis=0)
            vh = jnp.concatenate([vM[:, hs], vP[:, hs], vC[:, hs]], axis=0)
            for g in range(G):
                hq = h * G + g
                qs = slice(hq * HEAD_DIM, (hq + 1) * HEAD_DIM)
                s = lax.dot_general(q_ref[:, qs], kh, _NT, preferred_element_type=F32)
                p, _ = _softmax_sink(s, mask, s_ref[0, hq])
                o = jnp.dot(p.astype(BF16), vh, preferred_element_type=F32)
                o_ref[:, qs] = o.astype(BF16)

    def kv_spec(which, colblk):
        if which == 0:
            return pl.BlockSpec((B, KV), lambda n: (0, colblk))
        if which == 1:
            return pl.BlockSpec((B, KV), lambda n: (jnp.maximum(n - 1, 0), colblk))
        return pl.BlockSpec((B, KV), lambda n: (n, colblk))

    return _hosted_call(
        body, name="attn_fwd", grid=(NB,),
        in_specs=[pl.BlockSpec((B, Q), lambda n: (n, 0)),
                  kv_spec(0, kcol), kv_spec(1, kcol), kv_spec(2, kcol),
                  kv_spec(0, vcol), kv_spec(1, vcol), kv_spec(2, vcol),
                  pl.BlockSpec(memory_space=pltpu.SMEM)],
        out_specs=pl.BlockSpec((B, Q), lambda n: (n, 0)),
        out_shape=jax.ShapeDtypeStruct((T, Q), BF16), args=[qkr, qkr, qkr, qkr, z, z, z, sinks],
        sem=("arbitrary",), side=side)


def _attn_bwd(qkr, z, sinks, dao, Q, KV, n_meta, side=None):
    T = qkr.shape[0]
    B = ATT_BLOCK
    NB = T // B
    NQ = Q // HEAD_DIM
    NKV = KV // HEAD_DIM
    G = NQ // NKV
    kcol, vcol = Q // KV, (Q + KV) // KV
    assert NQ <= LANES

    def body(q_ref, do_ref, kM, kP, kC, vM, vP, vC, s_ref,
             dq_ref, dk_ref, dv_ref, dkm_ref, dvm_ref, ds_ref, ck_ref, cv_ref):
        n = pl.program_id(0)

        @pl.when(n == 0)
        def _():
            dkm_ref[...] = jnp.zeros_like(dkm_ref)
            dvm_ref[...] = jnp.zeros_like(dvm_ref)
            ds_ref[...] = jnp.zeros_like(ds_ref)
            ck_ref[...] = jnp.zeros_like(ck_ref)
            cv_ref[...] = jnp.zeros_like(cv_ref)

        @pl.when(n < NB)
        def _():
            mask = _attn_mask(n, n_meta)
            lane = lax.broadcasted_iota(jnp.int32, (1, LANES), 1)
            dsink = jnp.zeros((1, LANES), F32)
            for h in range(NKV):
                hs = slice(h * HEAD_DIM, (h + 1) * HEAD_DIM)
                kh = jnp.concatenate([kM[:, hs], kP[:, hs], kC[:, hs]], axis=0)
                vh = jnp.concatenate([vM[:, hs], vP[:, hs], vC[:, hs]], axis=0)
                dkh = jnp.zeros((3 * B, HEAD_DIM), F32)
                dvh = jnp.zeros((3 * B, HEAD_DIM), F32)
                for g in range(G):
                    hq = h * G + g
                    qs = slice(hq * HEAD_DIM, (hq + 1) * HEAD_DIM)
                    q = q_ref[:, qs]
                    do = do_ref[:, qs]
                    s = lax.dot_general(q, kh, _NT, preferred_element_type=F32)
                    p, psink = _softmax_sink(s, mask, s_ref[0, hq])
                    dp = lax.dot_general(do, vh, _NT, preferred_element_type=F32)
                    delta = jnp.sum(p * dp, axis=1, keepdims=True)
                    dsc = (p * (dp - delta)).astype(BF16)
                    dsink = dsink + jnp.where(lane == hq, -jnp.sum(psink * delta), 0.0)
                    dq_ref[:, qs] = jnp.dot(dsc, kh, preferred_element_type=F32)
                    dkh = dkh + lax.dot_general(dsc, q, _TN, preferred_element_type=F32)
                    dvh = dvh + lax.dot_general(p.astype(BF16), do, _TN, preferred_element_type=F32)
                dkm_ref[:, hs] += dkh[0:B]
                dvm_ref[:, hs] += dvh[0:B]
                dk_ref[:, hs] = ck_ref[:, hs] + dkh[B:2 * B]
                dv_ref[:, hs] = cv_ref[:, hs] + dvh[B:2 * B]
                ck_ref[:, hs] = dkh[2 * B:3 * B]
                cv_ref[:, hs] = dvh[2 * B:3 * B]
            ds_ref[...] += dsink

        @pl.when(n == NB)
        def _():
            dk_ref[...] = ck_ref[...]
            dv_ref[...] = cv_ref[...]

    last = NB - 1

    def kv_spec(which, colblk):
        if which == 0:
            return pl.BlockSpec((B, KV), lambda n: (0, colblk))
        if which == 1:
            return pl.BlockSpec((B, KV), lambda n: (jnp.clip(n - 1, 0, last), colblk))
        return pl.BlockSpec((B, KV), lambda n: (jnp.minimum(n, last), colblk))

    qspec = pl.BlockSpec((B, Q), lambda n: (jnp.minimum(n, last), 0))
    carry = pl.BlockSpec((B, KV), lambda n: (jnp.maximum(n - 1, 0), 0))
    const = pl.BlockSpec((B, KV), lambda n: (0, 0))
    return _hosted_call(
        body, name="attn_bwd", grid=(NB + 1,),
        in_specs=[qspec, qspec,
                  kv_spec(0, kcol), kv_spec(1, kcol), kv_spec(2, kcol),
                  kv_spec(0, vcol), kv_spec(1, vcol), kv_spec(2, vcol),
                  pl.BlockSpec(memory_space=pltpu.SMEM)],
        out_specs=[qspec, carry, carry, const, const, pl.BlockSpec((1, LANES), lambda n: (0, 0))],
        out_shape=[jax.ShapeDtypeStruct((T, Q), F32), jax.ShapeDtypeStruct((T, KV), F32),
                   jax.ShapeDtypeStruct((T, KV), F32), jax.ShapeDtypeStruct((B, KV), F32),
                   jax.ShapeDtypeStruct((B, KV), F32), jax.ShapeDtypeStruct((1, LANES), F32)],
        scratch_shapes=[pltpu.VMEM((B, KV), F32), pltpu.VMEM((B, KV), F32)],
        args=[qkr, dao, qkr, qkr, qkr, z, z, z, sinks], sem=("arbitrary",), side=side)


def _rope_bwd(dz, dq, dk, dv, dkm, dvm, cosf, sinf, Q, KV):
    T = dq.shape[0]
    B = ATT_BLOCK
    W = Q + 2 * KV

    def body(dq_ref, dk_ref, dv_ref, dkm_ref, dvm_ref, c_ref, s_ref, dz_in, o_ref):
        first = (pl.program_id(0) == 0).astype(F32)
        c = c_ref[...]
        s = s_ref[...]

        def unrope(d):
            reps = d.shape[1] // LANES
            return d * jnp.tile(c, (1, reps)) - _swap_halves(d) * jnp.tile(s, (1, reps))

        o_ref[:, 0:Q] = (unrope(dq_ref[...]) * (HEAD_DIM ** -0.5)).astype(BF16)
        o_ref[:, Q:Q + KV] = unrope(dk_ref[...] + first * dkm_ref[...]).astype(BF16)
        o_ref[:, Q + KV:W] = (dv_ref[...] + first * dvm_ref[...]).astype(BF16)

    kvs = pl.BlockSpec((B, KV), lambda i: (i, 0))
    const = pl.BlockSpec((B, KV), lambda i: (0, 0))
    tab = pl.BlockSpec((B, LANES), lambda i: (i, 0))
    return pl.pallas_call(
        body, name="rope_bwd", grid=(T // B,),
        in_specs=[pl.BlockSpec((B, Q), lambda i: (i, 0)), kvs, kvs, const, const, tab, tab, ANY],
        out_specs=pl.BlockSpec((B, W), lambda i: (i, 0)),
        out_shape=jax.ShapeDtypeStruct(dz.shape, dz.dtype), input_output_aliases={7: 0},
        compiler_params=_params(("parallel",)),
    )(dq, dk, dv, dkm, dvm, cosf, sinf, dz)


def _glu_rows(a_ref, g_ref):
    return a_ref[...].astype(F32) * jax.nn.sigmoid(g_ref[...].astype(F32))


def _conv_fwd(z, conv_w, conv_b, a_col, CV, side=None):
    T = z.shape[0]
    cw = _tile(CV, 512)
    tr = _tile(T, 384, CONV_HALO)
    H = CONV_HALO
    ab, gb = a_col // cw, (a_col + CV) // cw
    assert a_col % cw == 0 and CV % cw == 0
    hpt = tr // H

    def body(a_ref, g_ref, ap_ref, gp_ref, w_ref, b_ref, co_ref, cbuf):
        has_prev = (pl.program_id(0) > 0).astype(F32)
        cbuf[0:H, :] = _glu_rows(ap_ref, gp_ref) * has_prev
        cbuf[H:H + tr, :] = _glu_rows(a_ref, g_ref)
        acc = jnp.broadcast_to(b_ref[...], (tr, cw))
        for j in range(CONV_WIDTH):
            acc = acc + w_ref[j:j + 1, :] * cbuf[pl.ds(H - (CONV_WIDTH - 1) + j, tr), :]
        co_ref[...] = acc

    cur = lambda blk: pl.BlockSpec((tr, cw), lambda i, j: (i, blk + j))
    prev = lambda blk: pl.BlockSpec((H, cw), lambda i, j: (jnp.maximum(i * hpt - 1, 0), blk + j))
    return _hosted_call(
        body, name="conv_fwd", grid=(T // tr, CV // cw),
        in_specs=[cur(ab), cur(gb), prev(ab), prev(gb),
                  pl.BlockSpec((32, cw), lambda i, j: (0, j)), pl.BlockSpec((1, cw), lambda i, j: (0, j))],
        out_specs=pl.BlockSpec((tr, cw), lambda i, j: (i, j)),
        out_shape=jax.ShapeDtypeStruct((T, CV), F32),
        scratch_shapes=[pltpu.VMEM((tr + H, cw), F32)],
        args=[z, z, z, z, conv_w, conv_b], sem=("arbitrary", "arbitrary"), side=side)


def _ln_parts(x, g, b):
    mu = jnp.mean(x, axis=-1, keepdims=True)
    xc = x - mu
    r = lax.rsqrt(jnp.mean(xc * xc, axis=-1, keepdims=True) + EPS)
    xh = xc * r
    return xh, r, xh * g + b


def _ln_swish_fwd(co, g, b):
    T, CV = co.shape
    tr = _tile(T, 128, 16)

    def body(x_ref, g_ref, b_ref, o_ref):
        _, _, y = _ln_parts(x_ref[...], g_ref[...], b_ref[...])
        o_ref[...] = (y * jax.nn.sigmoid(y)).astype(BF16)

    row = pl.BlockSpec((tr, CV), lambda i: (i, 0))
    vec = pl.BlockSpec((1, CV), lambda i: (0, 0))
    return pl.pallas_call(
        body, name="ln_swish_fwd", grid=(T // tr,), in_specs=[row, vec, vec], out_specs=row,
        out_shape=jax.ShapeDtypeStruct((T, CV), BF16), compiler_params=_params(("parallel",)),
    )(co, g, b)


def _ln_swish_bwd(dcs, co, g, b):
    T, CV = co.shape
    tr = _tile(T, 128, 16)

    def body(d_ref, x_ref, g_ref, b_ref, dx_ref, dg_ref, db_ref):
        gain = g_ref[...]
        xh, r, y = _ln_parts(x_ref[...], gain, b_ref[...])
        sg = jax.nn.sigmoid(y)
        dy = d_ref[...].astype(F32) * (sg * (1.0 + y * (1.0 - sg)))
        dxh = dy * gain
        dx_ref[...] = r * (dxh - jnp.mean(dxh, axis=-1, keepdims=True)
                           - xh * jnp.mean(dxh * xh, axis=-1, keepdims=True))
        gpart = jnp.sum(dy * xh, axis=0, keepdims=True)
        bpart = jnp.sum(dy, axis=0, keepdims=True)

        @pl.when(pl.program_id(0) == 0)
        def _():
            dg_ref[...] = gpart
            db_ref[...] = bpart

        @pl.when(pl.program_id(0) > 0)
        def _():
            dg_ref[...] += gpart
            db_ref[...] += bpart

    row = pl.BlockSpec((tr, CV), lambda i: (i, 0))
    vec = pl.BlockSpec((1, CV), lambda i: (0, 0))
    return pl.pallas_call(
        body, name="ln_swish_bwd", grid=(T // tr,), in_specs=[row, row, vec, vec], out_specs=[row, vec, vec],
        out_shape=[jax.ShapeDtypeStruct((T, CV), F32), jax.ShapeDtypeStruct((1, CV), F32),
                   jax.ShapeDtypeStruct((1, CV), F32)],
        compiler_params=_params(("arbitrary",)),
    )(dcs, co, g, b)


def _conv_bwd(dco, z, conv_w, a_col, CV, side=None):
    T = z.shape[0]
    cw = _tile(CV, 512)
    tr = _tile(T, 384, CONV_HALO)
    H = CONV_HALO
    ab, gb = a_col // cw, (a_col + CV) // cw
    hpt = tr // H
    ni = T // tr
    last_halo = T // H - 1

    def body(d_ref, dn_ref, a_ref, g_ref, ap_ref, gp_ref, w_ref, dc_ref, dw_ref, db_ref, cbuf, dbuf):
        i = pl.program_id(1)
        has_prev = (i > 0).astype(F32)
        has_next = (i < ni - 1).astype(F32)
        d = d_ref[...]
        cbuf[0:H, :] = _glu_rows(ap_ref, gp_ref) * has_prev
        cbuf[H:H + tr, :] = _glu_rows(a_ref, g_ref)
        dbuf[0:tr, :] = d
        dbuf[tr:tr + H, :] = dn_ref[...] * has_next

        @pl.when(i == 0)
        def _():
            dw_ref[...] = jnp.zeros_like(dw_ref)
            db_ref[...] = jnp.zeros_like(db_ref)

        acc = jnp.zeros((tr, cw), F32)
        for j in range(CONV_WIDTH):
            acc = acc + w_ref[j:j + 1, :] * dbuf[pl.ds(CONV_WIDTH - 1 - j, tr), :]
            dw_ref[j:j + 1, :] += jnp.sum(d * cbuf[pl.ds(H - (CONV_WIDTH - 1) + j, tr), :], axis=0, keepdims=True)
        dc_ref[...] = acc
        db_ref[...] += jnp.sum(d, axis=0, keepdims=True)

    cur = lambda blk: pl.BlockSpec((tr, cw), lambda j, i: (i, blk + j))
    prev = lambda blk: pl.BlockSpec((H, cw), lambda j, i: (jnp.maximum(i * hpt - 1, 0), blk + j))
    nxt = pl.BlockSpec((H, cw), lambda j, i: (jnp.minimum((i + 1) * hpt, last_halo), j))
    return _hosted_call(
        body, name="conv_bwd", grid=(CV // cw, ni),
        in_specs=[cur(0), nxt, cur(ab), cur(gb), prev(ab), prev(gb), pl.BlockSpec((32, cw), lambda j, i: (0, j))],
        out_specs=[pl.BlockSpec((tr, cw), lambda j, i: (i, j)), pl.BlockSpec((32, cw), lambda j, i: (0, j)),
                   pl.BlockSpec((1, cw), lambda j, i: (0, j))],
        out_shape=[jax.ShapeDtypeStruct((T, CV), F32), jax.ShapeDtypeStruct((32, CV), F32),
                   jax.ShapeDtypeStruct((1, CV), F32)],
        scratch_shapes=[pltpu.VMEM((tr + H, cw), F32), pltpu.VMEM((tr + H, cw), F32)],
        args=[dco, dco, z, z, z, z, conv_w], sem=("arbitrary", "arbitrary"), side=side)


def _glu_bwd(dz, dc, z, a_col, CV):
    T = z.shape[0]
    cw = _tile(CV, 512)
    tr = _tile(T, 384, 16)
    nc = CV // cw
    ab = a_col // cw

    def body(dc_ref, a_ref, g_ref, dz_in, o_ref):
        j = pl.program_id(1)
        d = dc_ref[...]
        sg = jax.nn.sigmoid(g_ref[...].astype(F32))

        @pl.when(j < nc)
        def _():
            o_ref[...] = (d * sg).astype(BF16)

        @pl.when(j >= nc)
        def _():
            o_ref[...] = (d * a_ref[...].astype(F32) * sg * (1.0 - sg)).astype(BF16)

    return pl.pallas_call(
        body, name="glu_bwd", grid=(T // tr, 2 * nc),
        in_specs=[pl.BlockSpec((tr, cw), lambda i, j: (i, j % nc)),
                  pl.BlockSpec((tr, cw), lambda i, j: (i, ab + j % nc)),
                  pl.BlockSpec((tr, cw), lambda i, j: (i, ab + nc + j % nc)), ANY],
        out_specs=pl.BlockSpec((tr, cw), lambda i, j: (i, ab + j)),
        out_shape=jax.ShapeDtypeStruct(dz.shape, dz.dtype), input_output_aliases={3: 0},
        compiler_params=_params(("parallel", "parallel")),
    )(dc, z, z, dz)


def _gate_fwd(z, ab, ga_col, D):
    T = z.shape[0]
    cw = _tile(D, 512)
    tr = _tile(T, 384, 16)
    nd = D // cw
    gblk = ga_col // cw
    assert ga_col % cw == 0

    def body(ga_ref, gb_ref, a_ref, b_ref, o_ref):
        m = (jax.nn.sigmoid(ga_ref[...].astype(F32)) * a_ref[...].astype(F32)
             + jax.nn.sigmoid(gb_ref[...].astype(F32)) * b_ref[...].astype(F32))
        o_ref[...] = m.astype(BF16)

    blk = lambda off: pl.BlockSpec((tr, cw), lambda i, j: (i, off + j))
    return pl.pallas_call(
        body, name="gate_fwd", grid=(T // tr, nd),
        in_specs=[blk(gblk), blk(gblk + nd), blk(0), blk(nd)], out_specs=blk(0),
        out_shape=jax.ShapeDtypeStruct((T, D), BF16), compiler_params=_params(("parallel", "parallel")),
    )(z, z, ab, ab)


def _gate_bwd(dz, dmg, z, ab, ga_col, D):
    T = z.shape[0]
    cw = _tile(D, 512)
    tr = _tile(T, 384, 16)
    nd = D // cw
    gblk = ga_col // cw

    def body(dm_ref, gt_ref, ab_ref, dz_in, dgt_ref, dab_ref):
        dm = dm_ref[...].astype(F32)
        sg = jax.nn.sigmoid(gt_ref[...].astype(F32))
        dab_ref[...] = (dm * sg).astype(BF16)
        dgt_ref[...] = (dm * ab_ref[...].astype(F32) * sg * (1.0 - sg)).astype(BF16)

    return pl.pallas_call(
        body, name="gate_bwd", grid=(T // tr, 2 * nd),
        in_specs=[pl.BlockSpec((tr, cw), lambda i, j: (i, j % nd)),
                  pl.BlockSpec((tr, cw), lambda i, j: (i, gblk + j)),
                  pl.BlockSpec((tr, cw), lambda i, j: (i, j)), ANY],
        out_specs=[pl.BlockSpec((tr, cw), lambda i, j: (i, gblk + j)), pl.BlockSpec((tr, cw), lambda i, j: (i, j))],
        out_shape=[jax.ShapeDtypeStruct(dz.shape, dz.dtype), jax.ShapeDtypeStruct((T, 2 * D), BF16)],
        input_output_aliases={3: 0}, compiler_params=_params(("parallel", "parallel")),
    )(dmg, z, ab, dz)


EW_BLOCK_ELEMS = 512 * 1024


def _ew_block(T, W):
    cw = max(d for d in _divisors(W, LANES) if d <= 8192)
    rows = [d for d in _divisors(T, 16) if d * cw <= EW_BLOCK_ELEMS]
    return (max(rows) if rows else 16), cw


def _swiglu_fwd(gu, F):
    T = gu.shape[0]
    tr, cw = _ew_block(T, F)
    nf = F // cw

    def body(g_ref, u_ref, o_ref):
        g = g_ref[...].astype(F32)
        o_ref[...] = (g * jax.nn.sigmoid(g) * u_ref[...].astype(F32)).astype(BF16)

    blk = lambda off: pl.BlockSpec((tr, cw), lambda i, j: (i, off + j))
    return pl.pallas_call(
        body, name="swiglu_fwd", grid=(T // tr, nf), in_specs=[blk(0), blk(nf)], out_specs=blk(0),
        out_shape=jax.ShapeDtypeStruct((T, F), BF16), compiler_params=_params(("parallel", "parallel")),
    )(gu, gu)


def _swiglu_bwd(dact, gu, F):
    T = gu.shape[0]
    tr, cw = _ew_block(T, F)
    nf = F // cw

    def body(d_ref, g_ref, u_ref, o_ref):
        j = pl.program_id(1)
        d = d_ref[...].astype(F32)
        g = g_ref[...].astype(F32)
        sg = jax.nn.sigmoid(g)

        @pl.when(j < nf)
        def _():
            o_ref[...] = (d * u_ref[...].astype(F32) * sg * (1.0 + g * (1.0 - sg))).astype(BF16)

        @pl.when(j >= nf)
        def _():
            o_ref[...] = (d * g * sg).astype(BF16)

    half = lambda off: pl.BlockSpec((tr, cw), lambda i, j: (i, off + j % nf))
    return pl.pallas_call(
        body, name="swiglu_bwd", grid=(T // tr, 2 * nf), in_specs=[half(0), half(0), half(nf)],
        out_specs=pl.BlockSpec((tr, cw), lambda i, j: (i, j)),
        out_shape=jax.ShapeDtypeStruct((T, 2 * F), BF16), compiler_params=_params(("parallel", "parallel")),
    )(dact, gu, gu)


def _coords():
    return lax.axis_index("x"), lax.axis_index("y"), lax.axis_index("c")


def _flip(v, bit):
    return 1 - v if bit else v


def _chip_peer(k, x, y):
    px, py = _flip(x, (k >> 1) & 1), _flip(y, k & 1)
    return px, py, 2 * px + py


def _remote(src, dst, send_sem, recv_sem, device):
    return pltpu.make_async_remote_copy(src_ref=src, dst_ref=dst, send_sem=send_sem, recv_sem=recv_sem,
                                        device_id=device, device_id_type=MESH)


def _allgather_sum_small(buf):
    R = buf.shape[0]

    def body(x_ref, all_ref, sum_ref, send_sems, recv_sems, local_sem):
        x, y, c = _coords()
        me = 4 * x + 2 * y + c
        mine = pltpu.make_async_copy(x_ref, all_ref.at[me], local_sem)
        mine.start()
        sends = []
        for k in range(1, N_DEV):
            peer = (_flip(x, (k >> 2) & 1), _flip(y, (k >> 1) & 1), _flip(c, k & 1))
            cp = _remote(x_ref, all_ref.at[me], send_sems.at[k - 1], recv_sems.at[k - 1], peer)
            cp.start()
            sends.append(cp)
        for k in range(1, N_DEV):
            sender = 4 * _flip(x, (k >> 2) & 1) + 2 * _flip(y, (k >> 1) & 1) + _flip(c, k & 1)
            _remote(x_ref, all_ref.at[sender], send_sems.at[k - 1], recv_sems.at[k - 1], (x, y, c)).wait_recv()
        for cp in sends:
            cp.wait_send()
        mine.wait()
        acc = all_ref[0]
        for d in range(1, N_DEV):
            acc = acc + all_ref[d]
        sum_ref[...] = acc

    vm = pl.BlockSpec(memory_space=pltpu.VMEM)
    return pl.pallas_call(
        body, name="allgather_sum_small", in_specs=[vm], out_specs=[vm, vm],
        out_shape=[jax.ShapeDtypeStruct((N_DEV, R, LANES), F32), jax.ShapeDtypeStruct((R, LANES), F32)],
        scratch_shapes=[pltpu.SemaphoreType.DMA((N_DEV - 1,)), pltpu.SemaphoreType.DMA((N_DEV - 1,)),
                        pltpu.SemaphoreType.DMA],
        compiler_params=pltpu.CompilerParams(vmem_limit_bytes=VMEM_LIMIT),
    )(buf)


def _all_gather_weights(shards, axes):
    nw = len(shards)
    out_shapes = []
    for sh, ax in zip(shards, axes):
        rs, ns = sh.shape
        out_shapes.append(jax.ShapeDtypeStruct((rs, 4 * ns) if ax == 1 else (4 * rs, ns), sh.dtype))

    def body(*refs):
        ins, outs = refs[:nw], refs[nw:2 * nw]
        local_sems, send_sems, recv_sems = refs[2 * nw:]
        x, y, c = _coords()
        s = 2 * x + y

        def slot(w, shard, half):
            rs, ns = shards[w].shape
            hr = rs // 2
            if axes[w] == 1:
                rows = pl.ds(0, rs) if half is None else pl.ds(pl.multiple_of(half * hr, 16), hr)
                return outs[w].at[rows, pl.ds(pl.multiple_of(shard * ns, LANES), ns)]
            r0 = shard * rs if half is None else shard * rs + half * hr
            return outs[w].at[pl.ds(pl.multiple_of(r0, 16), rs if half is None else hr), :]

        def my_half(w):
            hr = shards[w].shape[0] // 2
            return ins[w].at[pl.ds(pl.multiple_of(c * hr, 16), hr), :]

        locals_ = [pltpu.make_async_copy(ins[w], slot(w, s, None), local_sems.at[w]) for w in range(nw)]
        for cp in locals_:
            cp.start()
        sends = []
        for w in range(nw):
            for k in (1, 2, 3):
                px, py, _ = _chip_peer(k, x, y)
                cp = _remote(my_half(w), slot(w, s, c), send_sems.at[6 * w + k - 1], recv_sems.at[6 * w + k - 1],
                             (px, py, c))
                cp.start()
                sends.append(cp)
        for w in range(nw):
            for k in (1, 2, 3):
                _, _, ps = _chip_peer(k, x, y)
                landed = slot(w, ps, c)
                _remote(landed, landed, send_sems.at[6 * w + k - 1], recv_sems.at[6 * w + k - 1], (x, y, c)).wait_recv()
                cp = _remote(landed, landed, send_sems.at[6 * w + 2 + k], recv_sems.at[6 * w + 2 + k], (x, y, 1 - c))
                cp.start()
                sends.append(cp)
        for w in range(nw):
            for k in (1, 2, 3):
                _, _, ps = _chip_peer(k, x, y)
                relayed = slot(w, ps, 1 - c)
                _remote(relayed, relayed, send_sems.at[6 * w + 2 + k], recv_sems.at[6 * w + 2 + k], (x, y, c)).wait_recv()
        for cp in sends:
            cp.wait_send()
        for cp in locals_:
            cp.wait()

    return pl.pallas_call(
        body, name="all_gather_weights", in_specs=[ANY] * nw, out_specs=[ANY] * nw, out_shape=out_shapes,
        scratch_shapes=[pltpu.SemaphoreType.DMA((nw,)), pltpu.SemaphoreType.DMA((6 * nw,)),
                        pltpu.SemaphoreType.DMA((6 * nw,))],
    )(*shards)


def _shard_view(ref, axis, shard):
    r, n = ref.shape
    if axis == 1:
        return ref.at[:, pl.ds(pl.multiple_of(shard * (n // 4), LANES), n // 4)]
    return ref.at[pl.ds(pl.multiple_of(shard * (r // 4), 16), r // 4), :]


def _rs_pair_send(dws, axes):
    nw = len(dws)

    def body(*refs):
        ins, outs = refs[:nw], refs[nw:2 * nw]
        send_sems, recv_sems = refs[2 * nw:]
        x, y, c = _coords()
        cps = [_remote(_half_view(ins[w], axes[w], 1 - c), outs[w], send_sems.at[w], recv_sems.at[w], (x, y, 1 - c))
               for w in range(nw)]
        for cp in cps:
            cp.start()
        for cp in cps:
            cp.wait()

    return pl.pallas_call(
        body, name="rs_pair_send", in_specs=[ANY] * nw, out_specs=[ANY] * nw,
        out_shape=[jax.ShapeDtypeStruct(_half_shape(d.shape, a), d.dtype) for d, a in zip(dws, axes)],
        scratch_shapes=[pltpu.SemaphoreType.DMA((nw,)), pltpu.SemaphoreType.DMA((nw,))],
    )(*dws)


def _rs_chip_send(ps, axes):
    nw = len(ps)

    def body(*refs):
        ins, outs = refs[:nw], refs[nw:2 * nw]
        send_sems, recv_sems = refs[2 * nw:]
        x, y, c = _coords()
        cps = []
        for w in range(nw):
            for k in (1, 2, 3):
                px, py, pshard = _chip_peer(k, x, y)
                cps.append(_remote(_shard_view(ins[w], axes[w], pshard), outs[w].at[k - 1],
                                   send_sems.at[3 * w + k - 1], recv_sems.at[3 * w + k - 1], (px, py, c)))
        for cp in cps:
            cp.start()
        for cp in cps:
            cp.wait()

    return pl.pallas_call(
        body, name="rs_chip_send", in_specs=[ANY] * nw, out_specs=[ANY] * nw,
        out_shape=[jax.ShapeDtypeStruct((3,) + _shard_shape(p.shape, a), p.dtype) for p, a in zip(ps, axes)],
        scratch_shapes=[pltpu.SemaphoreType.DMA((3 * nw,)), pltpu.SemaphoreType.DMA((3 * nw,))],
    )(*ps)


def _rs_pair_swap(g2s):
    nw = len(g2s)

    def body(*refs):
        ins, outs = refs[:nw], refs[nw:2 * nw]
        send_sems, recv_sems = refs[2 * nw:]
        x, y, c = _coords()
        cps = [_remote(outs[w].at[c], outs[w].at[c], send_sems.at[w], recv_sems.at[w], (x, y, 1 - c))
               for w in range(nw)]
        for cp in cps:
            cp.start()
        for w, cp in enumerate(cps):
            cp.wait_send()
            _remote(outs[w].at[1 - c], outs[w].at[1 - c], send_sems.at[w], recv_sems.at[w], (x, y, c)).wait_recv()

    return pl.pallas_call(
        body, name="rs_pair_swap", in_specs=[ANY] * nw, out_specs=[ANY] * nw,
        out_shape=[jax.ShapeDtypeStruct(g.shape, g.dtype) for g in g2s],
        input_output_aliases={w: w for w in range(nw)},
        scratch_shapes=[pltpu.SemaphoreType.DMA((nw,)), pltpu.SemaphoreType.DMA((nw,))],
    )(*g2s)


def _comm_call(name, *ops):
    side = _Side(*ops)
    _hosted_call(lambda: None, name=name, grid=(), in_specs=[], out_specs=[], out_shape=[], args=[], side=side)


def _split(n, fractions, mult=16):
    out, pos = [], 0
    for f in fractions[:-1]:
        size = max(mult, int(round(n * f / mult)) * mult)
        out.append((pos, size))
        pos += size
    assert n - pos >= mult and (n - pos) % mult == 0, (n, fractions)
    return out + [(pos, n - pos)]


def _full_shape(shard, axis):
    rs, ns = shard.shape
    return (rs, 4 * ns) if axis == 1 else (4 * rs, ns)


class _GatherOp(_Op):
    def __init__(self, entries):
        self.entries = entries
        n = len(entries)
        self.ins = [a for e in entries for a in (e[0], e[1])]
        self.outs = [jax.ShapeDtypeStruct(e[1].shape, e[1].dtype) for e in entries]
        self.alias = {2 * i + 1: i for i in range(n)}
        self.sems = [pltpu.SemaphoreType.DMA((n,)), pltpu.SemaphoreType.DMA((6 * n,)),
                     pltpu.SemaphoreType.DMA((6 * n,))]

    def _copies(self, ins, outs, sems):
        local_sems, send_sems, recv_sems = sems
        x, y, c = _coords()
        s = 2 * x + y
        own, ici, landed, relay, relayed = [], [], [], [], []
        for i, (shard, _, axis, r0, nr, with_own) in enumerate(self.entries):
            rs, ns = shard.shape
            hr = rs // 2
            src, full = ins[2 * i], outs[i]

            def place(sh, half, axis=axis, rs=rs, ns=ns, hr=hr, r0=r0, nr=nr, full=full):
                if axis == 1:
                    return full.at[pl.ds(pl.multiple_of(half * hr + r0, 16), nr), pl.ds(pl.multiple_of(sh * ns, LANES), ns)]
                return full.at[pl.ds(pl.multiple_of(sh * rs + half * hr + r0, 16), nr), :]

            if with_own:
                whole = (full.at[:, pl.ds(pl.multiple_of(s * ns, LANES), ns)] if axis == 1
                         else full.at[pl.ds(pl.multiple_of(s * rs, 16), rs), :])
                own.append(pltpu.make_async_copy(src, whole, local_sems.at[i]))
            for k in (1, 2, 3):
                px, py, ps = _chip_peer(k, x, y)
                a, b = 6 * i + k - 1, 6 * i + 2 + k
                ici.append(_remote(src.at[pl.ds(pl.multiple_of(c * hr + r0, 16), nr), :], place(s, c),
                                   send_sems.at[a], recv_sems.at[a], (px, py, c)))
                landed.append(_remote(place(ps, c), place(ps, c), send_sems.at[a], recv_sems.at[a], (x, y, c)))
                relay.append(_remote(place(ps, c), place(ps, c), send_sems.at[b], recv_sems.at[b], (x, y, 1 - c)))
                relayed.append(_remote(place(ps, 1 - c), place(ps, 1 - c), send_sems.at[b], recv_sems.at[b], (x, y, c)))
        return own, ici, landed, relay, relayed

    def start(self, ins, outs, sems):
        own, ici, _, _, _ = self._copies(ins, outs, sems)
        for cp in own + ici:
            cp.start()

    def finish(self, ins, outs, sems):
        own, ici, landed, relay, relayed = self._copies(ins, outs, sems)
        for arrival, onward in zip(landed, relay):
            arrival.wait_recv()
            onward.start()
        for cp in relayed:
            cp.wait_recv()
        for cp in ici + relay:
            cp.wait_send()
        for cp in own:
            cp.wait()


def _half_view(ref, axis, half):
    r, n = ref.shape
    if axis == 1:
        return ref.at[pl.ds(pl.multiple_of(half * (r // 2), 16), r // 2), :]
    return ref.at[:, pl.ds(pl.multiple_of(half * (n // 2), LANES), n // 2)]


def _half_shape(shape, axis):
    r, n = shape
    return (r // 2, n) if axis == 1 else (r, n // 2)


def _shard_shape(shape, axis):
    r, n = shape
    return (r, n // 4) if axis == 1 else (r // 4, n)


class _PairSendOp(_Op):
    def __init__(self, dws, axes):
        self.axes = axes
        self.ins = list(dws)
        self.outs = [jax.ShapeDtypeStruct(_half_shape(d.shape, a), d.dtype) for d, a in zip(dws, axes)]
        self.sems = [pltpu.SemaphoreType.DMA((len(dws),)), pltpu.SemaphoreType.DMA((len(dws),))]

    def _copies(self, ins, outs, sems):
        x, y, c = _coords()
        return [_remote(_half_view(ins[w], self.axes[w], 1 - c), outs[w], sems[0].at[w], sems[1].at[w], (x, y, 1 - c))
                for w in range(len(ins))]

    def start(self, ins, outs, sems):
        for cp in self._copies(ins, outs, sems):
            cp.start()

    def finish(self, ins, outs, sems):
        for cp in self._copies(ins, outs, sems):
            cp.wait()


class _ChipSendOp(_Op):
    def __init__(self, entries):
        self.entries = entries
        n = len(entries)
        self.ins = [a for e in entries for a in (e[0], e[1])]
        self.outs = [jax.ShapeDtypeStruct(e[1].shape, e[1].dtype) for e in entries]
        self.alias = {2 * i + 1: i for i in range(n)}
        self.sems = [pltpu.SemaphoreType.DMA((3 * n,)), pltpu.SemaphoreType.DMA((3 * n,))]

    def _copies(self, ins, outs, sems):
        x, y, c = _coords()
        cps = []
        for i, (p, _, axis, r0, nr) in enumerate(self.entries):
            r, n = p.shape
            for k in (1, 2, 3):
                px, py, ps = _chip_peer(k, x, y)
                if axis == 1:
                    src = ins[2 * i].at[pl.ds(r0, nr), pl.ds(pl.multiple_of(ps * (n // 4), LANES), n // 4)]
                else:
                    src = ins[2 * i].at[pl.ds(pl.multiple_of(ps * (r // 4) + r0, 16), nr), :]
                cps.append(_remote(src, outs[i].at[k - 1, pl.ds(r0, nr), :], sems[0].at[3 * i + k - 1],
                                   sems[1].at[3 * i + k - 1], (px, py, c)))
        return cps

    def start(self, ins, outs, sems):
        for cp in self._copies(ins, outs, sems):
            cp.start()

    def finish(self, ins, outs, sems):
        for cp in self._copies(ins, outs, sems):
            cp.wait()


class _PairSwapOp(_Op):
    def __init__(self, g2s):
        n = len(g2s)
        self.ins = list(g2s)
        self.outs = [jax.ShapeDtypeStruct(g.shape, g.dtype) for g in g2s]
        self.alias = {i: i for i in range(n)}
        self.sems = [pltpu.SemaphoreType.DMA((n,)), pltpu.SemaphoreType.DMA((n,))]

    def start(self, ins, outs, sems):
        x, y, c = _coords()
        for w in range(len(outs)):
            _remote(outs[w].at[c], outs[w].at[c], sems[0].at[w], sems[1].at[w], (x, y, 1 - c)).start()

    def finish(self, ins, outs, sems):
        x, y, c = _coords()
        for w in range(len(outs)):
            _remote(outs[w].at[c], outs[w].at[c], sems[0].at[w], sems[1].at[w], (x, y, 1 - c)).wait_send()
            _remote(outs[w].at[1 - c], outs[w].at[1 - c], sems[0].at[w], sems[1].at[w], (x, y, c)).wait_recv()


def _ew_tiles(shape):
    r, n = shape
    return _tile(r, 256, 16), _tile(n, 1024)


def _pair_add(name, dw, sib, axis, c1):
    hr, hn = sib.shape
    tr, tn = _ew_tiles((hr, hn))
    ni, nj = hr // tr, hn // tn
    if axis == 1:
        own = lambda i, j, c: (c[0] * ni + i, j)
    else:
        own = lambda i, j, c: (i, c[0] * nj + j)

    def body(c_ref, a_ref, b_ref, o_ref):
        o_ref[...] = (a_ref[...].astype(F32) + b_ref[...].astype(F32)).astype(BF16)

    blk = pl.BlockSpec((tr, tn), lambda i, j, c: (i, j))
    return pl.pallas_call(
        body, name=name,
        grid_spec=pltpu.PrefetchScalarGridSpec(
            num_scalar_prefetch=1, grid=(ni, nj), in_specs=[pl.BlockSpec((tr, tn), own), blk], out_specs=blk),
        out_shape=jax.ShapeDtypeStruct((hr, hn), BF16), compiler_params=_params(("parallel", "parallel")),
    )(c1, dw, sib)


def _shard_sum(name, p, arr, axis, sc):
    sr, sn = arr.shape[1:]
    tr, tn = _ew_tiles((sr, sn))
    ni, nj = sr // tr, sn // tn
    if axis == 1:
        own = lambda i, j, sc: (i, sc[0] * nj + j)
    else:
        own = lambda i, j, sc: (sc[0] * ni + i, j)

    def body(sc_ref, p_ref, a_ref, o_ref):
        o_ref[...] = ((p_ref[...].astype(F32) + a_ref[0].astype(F32)) + a_ref[1].astype(F32)) + a_ref[2].astype(F32)

    return pl.pallas_call(
        body, name=name,
        grid_spec=pltpu.PrefetchScalarGridSpec(
            num_scalar_prefetch=1, grid=(ni, nj),
            in_specs=[pl.BlockSpec((tr, tn), own), pl.BlockSpec((3, tr, tn), lambda i, j, sc: (0, i, j))],
            out_specs=pl.BlockSpec((None, tr, tn), lambda i, j, sc: (sc[1], i, j))),
        out_shape=jax.ShapeDtypeStruct((2, sr, sn), F32), compiler_params=_params(("parallel", "parallel")),
    )(sc, p, arr)


def _adamw_math(w, g, m, v):
    m = ADAM_B1 * m + (1.0 - ADAM_B1) * g
    v = ADAM_B2 * v + (1.0 - ADAM_B2) * (g * g)
    m_hat = m / (1.0 - ADAM_B1 ** ADAM_STEP)
    v_hat = v / (1.0 - ADAM_B2 ** ADAM_STEP)
    delta = -ADAM_LR * (m_hat / (jnp.sqrt(v_hat) + ADAM_EPS) + ADAM_WD * w)
    return delta, m, v


def _adamw_halves(name, g2, w, m, v, axis):
    sr, sn = g2.shape[1:]
    tr, tn = _ew_tiles((sr, sn))
    ni, nj = sr // tr, sn // tn
    if axis == 1:
        full = lambda h, i, j: (h * ni + i, j)
    else:
        full = lambda h, i, j: (i, h * nj + j)

    def body(g_ref, w_ref, m_ref, v_ref, go_ref, d_ref, mo_ref, vo_ref):
        g = g_ref[...]
        delta, m2, v2 = _adamw_math(w_ref[...], g, m_ref[...], v_ref[...])
        go_ref[...] = g
        d_ref[...] = delta
        mo_ref[...] = m2
        vo_ref[...] = v2

    fb = pl.BlockSpec((tr, tn), full)
    shp = jax.ShapeDtypeStruct(w.shape, F32)
    return pl.pallas_call(
        body, name=name, grid=(2, ni, nj),
        in_specs=[pl.BlockSpec((None, tr, tn), lambda h, i, j: (h, i, j)), fb, fb, fb], out_specs=[fb] * 4,
        out_shape=[shp] * 4, compiler_params=_params(("parallel", "parallel", "parallel")),
    )(g2, w, m, v)


def _adamw_flat(name, g, w, m, v):
    R = g.shape[0]
    tr = _tile(R, 512, 8)

    def body(g_ref, w_ref, m_ref, v_ref, d_ref, mo_ref, vo_ref):
        delta, m2, v2 = _adamw_math(w_ref[...], g_ref[...], m_ref[...], v_ref[...])
        d_ref[...] = delta
        mo_ref[...] = m2
        vo_ref[...] = v2

    blk = pl.BlockSpec((tr, LANES), lambda i: (i, 0))
    shp = jax.ShapeDtypeStruct((R, LANES), F32)
    return pl.pallas_call(
        body, name=name, grid=(R // tr,), in_specs=[blk] * 4, out_specs=[blk] * 3, out_shape=[shp] * 3,
        compiler_params=_params(("parallel",)),
    )(g, w, m, v)


def _pack(parts):
    flat = jnp.concatenate([p.astype(F32).reshape(-1) for p in parts])
    n = flat.shape[0]
    total = -(-n // (8 * LANES)) * (8 * LANES)
    return jnp.pad(flat, (0, total - n)).reshape(total // LANES, LANES)


def _unpack(buf, shapes):
    flat = buf.reshape(-1)
    out, pos = [], 0
    for shp in shapes:
        n = 1
        for d in shp:
            n *= d
        out.append(flat[pos:pos + n].reshape(shp))
        pos += n
    return out


_WEIGHTS = ("meta_tokens", "mix_norm_g", "w_in", "b_in", "attn_sinks", "conv_w", "conv_b", "conv_ln_g",
            "conv_ln_b", "w_attn_o", "w_conv_o", "b_conv_o", "w_out", "ffn_norm_g", "w_gate_up", "w_down",
            "final_norm_g")
_BIG = ("w_in", "w_attn_o", "w_conv_o", "w_out", "w_gate_up", "w_down")
_BIG_AXES = (1, 1, 1, 0, 1, 0)


def kernel(x, meta_tokens, mix_norm_g, w_in, b_in, attn_sinks, conv_w, conv_b, conv_ln_g, conv_ln_b, w_attn_o, w_conv_o, b_conv_o, w_out, ffn_norm_g, w_gate_up, w_down, final_norm_g, loss_target, m_meta_tokens, m_mix_norm_g, m_w_in, m_b_in, m_attn_sinks, m_conv_w, m_conv_b, m_conv_ln_g, m_conv_ln_b, m_w_attn_o, m_w_conv_o, m_b_conv_o, m_w_out, m_ffn_norm_g, m_w_gate_up, m_w_down, m_final_norm_g, v_meta_tokens, v_mix_norm_g, v_w_in, v_b_in, v_attn_sinks, v_conv_w, v_conv_b, v_conv_ln_g, v_conv_ln_b, v_w_attn_o, v_w_conv_o, v_b_conv_o, v_w_out, v_ffn_norm_g, v_w_gate_up, v_w_down, v_final_norm_g):
    W = dict(meta_tokens=meta_tokens, mix_norm_g=mix_norm_g, w_in=w_in, b_in=b_in, attn_sinks=attn_sinks,
             conv_w=conv_w, conv_b=conv_b, conv_ln_g=conv_ln_g, conv_ln_b=conv_ln_b, w_attn_o=w_attn_o,
             w_conv_o=w_conv_o, b_conv_o=b_conv_o, w_out=w_out, ffn_norm_g=ffn_norm_g, w_gate_up=w_gate_up,
             w_down=w_down, final_norm_g=final_norm_g)
    M1 = dict(meta_tokens=m_meta_tokens, mix_norm_g=m_mix_norm_g, w_in=m_w_in, b_in=m_b_in,
              attn_sinks=m_attn_sinks, conv_w=m_conv_w, conv_b=m_conv_b, conv_ln_g=m_conv_ln_g,
              conv_ln_b=m_conv_ln_b, w_attn_o=m_w_attn_o, w_conv_o=m_w_conv_o, b_conv_o=m_b_conv_o,
              w_out=m_w_out, ffn_norm_g=m_ffn_norm_g, w_gate_up=m_w_gate_up, w_down=m_w_down,
              final_norm_g=m_final_norm_g)
    M2 = dict(meta_tokens=v_meta_tokens, mix_norm_g=v_mix_norm_g, w_in=v_w_in, b_in=v_b_in,
              attn_sinks=v_attn_sinks, conv_w=v_conv_w, conv_b=v_conv_b, conv_ln_g=v_conv_ln_g,
              conv_ln_b=v_conv_ln_b, w_attn_o=v_w_attn_o, w_conv_o=v_w_conv_o, b_conv_o=v_b_conv_o,
              w_out=v_w_out, ffn_norm_g=v_ffn_norm_g, w_gate_up=v_w_gate_up, w_down=v_w_down,
              final_norm_g=v_final_norm_g)

    SEQ, D = x.shape[1], x.shape[2]
    NM, Dq = meta_tokens.shape
    Q, CV, IN = w_attn_o.shape[1], w_conv_o.shape[1], b_in.shape[1]
    F = 4 * w_down.shape[1]
    KV = (IN - Q - 2 * CV - 2 * D) // 2
    NQ = attn_sinks.shape[1]
    CVq = conv_w.shape[3]
    T = -(-(NM + SEQ) // ATT_BLOCK) * ATT_BLOCK
    a_col = Q + 2 * KV
    ga_col = a_col + 2 * CV
    assert Q == NQ * HEAD_DIM and Dq * 4 == D and CVq * 4 == CV

    cx, cy, cc = _coords()
    shard = 2 * cx + cy

    small_shapes = [(NM, Dq), (CONV_WIDTH, CVq)]
    gathered, _ = _allgather_sum_small(_pack([meta_tokens, conv_w.reshape(CONV_WIDTH, CVq)]))
    per_chip = [_unpack(gathered[2 * sh], small_shapes) for sh in range(4)]
    meta_full = jnp.concatenate([p[0] for p in per_chip], axis=1)
    taps = jnp.concatenate([p[1] for p in per_chip], axis=1)
    taps = jnp.concatenate([taps, jnp.zeros((32 - CONV_WIDTH, CV), F32)], axis=0)

    win, wao, wco, wout, wgu, wdn = _all_gather_weights([W[n][0].astype(BF16) for n in _BIG], _BIG_AXES)

    pad = T - NM - SEQ
    h0 = jnp.concatenate([meta_full, x[0], jnp.zeros((pad, D), F32)], axis=0)
    tgt = jnp.concatenate([jnp.zeros((NM, D), F32), loss_target[0], jnp.zeros((pad, D), F32)], axis=0)
    cosf, sinf = _rope_tables(T)

    u1 = _rms_fwd("rms_mix", h0, mix_norm_g)
    z = _mm("mm_in", u1, win, "nn", T, IN, D, bias=b_in)
    qkr = _rope_fwd(z, cosf, sinf, Q, KV)
    ao = _attn_fwd(qkr, z, attn_sinks, Q, KV, NM)
    co = _conv_fwd(z, taps, conv_b, a_col, CV)
    cs = _ln_swish_fwd(co, conv_ln_g, conv_ln_b)
    ab = _mm("mm_attn_o", ao, wao, "nn", T, D, Q, out=lax.empty((T, 2 * D), BF16))
    ab = _mm("mm_conv_o", cs, wco, "nn", T, D, CV, bias=b_conv_o, out=ab, out_off=(0, D))
    mg = _gate_fwd(z, ab, ga_col, D)
    h1 = _mm("mm_out", mg, wout, "nn", T, D, D, res=h0, out_dtype=F32)
    u2 = _rms_fwd("rms_ffn", h1, ffn_norm_g)
    gu = _mm("mm_gate_up", u2, wgu, "nn", T, 2 * F, D)
    act = _swiglu_fwd(gu, F)
    h2 = _mm("mm_down", act, wdn, "nn", T, D, F, res=h1, out_dtype=F32)
    loss11, dh2, dh2b, d_final_g = _final_loss(h2, final_norm_g.reshape(1, D), tgt, NM, NM + SEQ)

    dact = _mm("mm_d_act", dh2b, wdn, "nt", T, F, D)
    dw_down = _mm("mm_dw_down", act, dh2b, "tn", F, D, T)
    dgu = _swiglu_bwd(dact, gu, F)
    du2 = _mm("mm_d_u2", dgu, wgu, "nt", T, D, 2 * F, out_dtype=F32)
    dw_gate_up = _mm("mm_dw_gate_up", u2, dgu, "tn", D, 2 * F, T)
    dh1, dh1b, d_ffn_g = _rms_bwd("rms_ffn_bwd", du2, h1, ffn_norm_g, dh2)
    dmg = _mm("mm_d_merged", dh1b, wout, "nt", T, D, D)
    dw_out = _mm("mm_dw_out", mg, dh1b, "tn", D, D, T)
    dz, dab = _gate_bwd(lax.empty((T, IN), BF16), dmg, z, ab, ga_col, D)
    dao = _mm("mm_d_attn", dab, wao, "nt", T, Q, D)
    dw_attn_o = _mm("mm_dw_attn_o", ao, dab, "tn", Q, D, T)
    dcs = _mm("mm_d_conv", dab, wco, "nt", T, CV, D, a_off=(0, D))
    dw_conv_o = _mm("mm_dw_conv_o", cs, dab, "tn", CV, D, T, b_off=(0, D))
    d_b_conv_o = _colsum("colsum_b_conv_o", dab, width=D, col_off=D)
    dco, d_ln_g, d_ln_b = _ln_swish_bwd(dcs, co, conv_ln_g, conv_ln_b)
    dc, d_taps, d_conv_b = _conv_bwd(dco, z, taps, a_col, CV)
    dz = _glu_bwd(dz, dc, z, a_col, CV)
    dq, dk, dv, dkm, dvm, d_sinks = _attn_bwd(qkr, z, attn_sinks, dao, Q, KV, NM)
    dz = _rope_bwd(dz, dq, dk, dv, dkm, dvm, cosf, sinf, Q, KV)
    du1 = _mm("mm_d_u1", dz, win, "nt", T, D, IN, out_dtype=F32)
    dw_in = _mm("mm_dw_in", u1, dz, "tn", D, IN, T)
    d_b_in = _colsum("colsum_b_in", dz)
    dh0, _, d_mix_g = _rms_bwd("rms_mix_bwd", du1, h0, mix_norm_g, dh1)
    grad_x = dh0[NM:NM + SEQ][None]

    red_shapes = [(1, 1), (1, D), (1, IN), (1, NQ), (1, CV), (1, CV), (1, CV), (1, D), (1, D), (1, D),
                  (NM, D), (CONV_WIDTH, CV)]
    _, red = _allgather_sum_small(_pack([
        loss11, d_mix_g, d_b_in, d_sinks[:, :NQ], d_conv_b, d_ln_g, d_ln_b, d_b_conv_o, d_ffn_g, d_final_g,
        dh0[:NM], d_taps[:CONV_WIDTH]]))
    (loss, g_mix, g_b_in, g_sinks, g_conv_b, g_ln_g, g_ln_b, g_b_conv_o, g_ffn, g_final, g_meta,
     g_taps) = _unpack(red, red_shapes)
    G = dict(
        meta_tokens=lax.dynamic_slice(g_meta, (0, shard * Dq), (NM, Dq)),
        mix_norm_g=g_mix, b_in=g_b_in, attn_sinks=g_sinks,
        conv_w=lax.dynamic_slice(g_taps, (0, shard * CVq), (CONV_WIDTH, CVq)).reshape(conv_w.shape),
        conv_b=g_conv_b, conv_ln_g=g_ln_g, conv_ln_b=g_ln_b, b_conv_o=g_b_conv_o, ffn_norm_g=g_ffn,
        final_norm_g=g_final.reshape(final_norm_g.shape))

    dws = [dw_in, dw_attn_o, dw_conv_o, dw_out, dw_gate_up, dw_down]
    c1 = jnp.reshape(cc, (1,)).astype(jnp.int32)
    sc = jnp.stack([shard, cc]).astype(jnp.int32)
    sibs = _rs_pair_send(dws, _BIG_AXES)
    pairs = [_pair_add(f"pair_add_{n}", dws[i], sibs[i], _BIG_AXES[i], c1) for i, n in enumerate(_BIG)]
    arrived = _rs_chip_send(pairs, _BIG_AXES)
    halves = [_shard_sum(f"shard_sum_{n}", pairs[i], arrived[i], _BIG_AXES[i], sc) for i, n in enumerate(_BIG)]
    halves = _rs_pair_swap(halves)

    D_, NM_, NV_ = {}, {}, {}
    for i, n in enumerate(_BIG):
        g, d, m2, v2 = _adamw_halves(f"adamw_{n}", halves[i], W[n][0], M1[n][0], M2[n][0], _BIG_AXES[i])
        G[n], D_[n], NM_[n], NV_[n] = g[None], d[None], m2[None], v2[None]

    small = [n for n in _WEIGHTS if n not in _BIG]
    shapes = [W[n].shape for n in small]
    d_s, m_s, v_s = _adamw_flat("adamw_small", _pack([G[n] for n in small]), _pack([W[n] for n in small]),
                                _pack([M1[n] for n in small]), _pack([M2[n] for n in small]))
    for n, d, m2, v2 in zip(small, _unpack(d_s, shapes), _unpack(m_s, shapes), _unpack(v_s, shapes)):
        D_[n], NM_[n], NV_[n] = d, m2, v2

    return (loss.reshape(()), grad_x, *[G[n] for n in _WEIGHTS], *[D_[n] for n in _WEIGHTS],
            *[NM_[n] for n in _WEIGHTS], *[NV_[n] for n in _WEIGHTS])
```
